```python
import jax, jax.numpy as jnp
from jax import lax
import numpy as np

D_MODEL = 1024
BATCH = 2
SEQ = 8192
DEPTH = 4

N_MIXERS = 2
N_HGRN_LAYERS = (DEPTH + 1) // 2
N_MLA_LAYERS = DEPTH // 2
RMS_EPS = 1e-6

HGRN_EXPAND = 128
HGRN_HEADS = D_MODEL // HGRN_EXPAND
HGRN_VDIM = D_MODEL // HGRN_HEADS
HGRN_CHUNK = 64

MLA_HEADS = D_MODEL // 128
MLA_NOPE = 128
MLA_ROPE = 64
MLA_QK_HEAD = MLA_NOPE + MLA_ROPE
MLA_V_HEAD = 128
MLA_Q_LORA = D_MODEL // 4
MLA_KV_LORA = D_MODEL // 4
ROPE_THETA = 10000.0
Q_BLOCK = 128

D_FF = 2816
FFN_CONV = 3

kernel_name = "hybrid_hgrn2_mla_convffn_trunk"


def _rmsnorm(x, gain):
    xf = x.astype(jnp.float32)
    y = xf * lax.rsqrt(jnp.mean(xf * xf, axis=-1, keepdims=True) + RMS_EPS)
    return (y * gain.astype(jnp.float32)).astype(x.dtype)


def _chunk_gated_recurrence(q, k, v, log_f):
    B, S, H, Dk = q.shape
    Dv = v.shape[-1]
    C = HGRN_CHUNK
    N = S // C

    def to_chunks(t):
        return t.astype(jnp.float32).reshape(B, N, C, H, t.shape[-1]).transpose(1, 0, 3, 2, 4)

    causal = jnp.tril(jnp.ones((C, C), dtype=bool))[:, :, None]

    def step(state, inp):
        qc, kc, vc, gc = inp
        G = jnp.cumsum(gc, axis=2)
        o_inter = jnp.einsum("bhtd,bhde->bhte", qc * jnp.exp(G), state)
        diff = G[:, :, :, None, :] - G[:, :, None, :, :]
        decay = jnp.where(causal, jnp.exp(jnp.where(causal, diff, 0.0)), 0.0)
        scores = jnp.einsum("bhtd,bhsd,bhtsd->bhts", qc, kc, decay)
        o_intra = jnp.einsum("bhts,bhse->bhte", scores, vc)
        G_last = G[:, :, -1:, :]
        new_state = state * jnp.exp(G_last[:, :, 0, :, None]) + jnp.einsum(
            "bhsd,bhse->bhde", kc * jnp.exp(G_last - G), vc)
        return new_state, o_inter + o_intra

    state0 = jnp.zeros((B, H, Dk, Dv), jnp.float32)
    _, o = lax.scan(step, state0, (to_chunks(q), to_chunks(k), to_chunks(v), to_chunks(log_f)))
    return o.transpose(1, 0, 3, 2, 4).reshape(B, S, H, Dv)


def _hgrn2_mixer(h, w_in, lower_bound, out_norm, w_out):
    B, S, _ = h.shape
    q, f, i, g = jnp.split(h @ w_in, 4, axis=-1)

    def heads(t):
        return t.reshape(B, S, HGRN_HEADS, -1)

    lb = lower_bound.astype(jnp.float32)
    forget = lb + (1.0 - lb) * jax.nn.sigmoid(f.astype(jnp.float32))
    key_in = 1.0 - forget
    log_f = jnp.log(forget)
    o = _chunk_gated_recurrence(heads(jax.nn.silu(q)), heads(key_in), heads(i), heads(log_f))
    o = _rmsnorm(o, out_norm) * jax.nn.silu(heads(g.astype(jnp.float32)))
    return o.reshape(B, S, D_MODEL).astype(h.dtype) @ w_out


def _rope_tail(x, cos, sin):
    x_nope, x1, x2 = jnp.split(x, [MLA_NOPE, MLA_NOPE + MLA_ROPE // 2], axis=-1)
    xf1 = x1.astype(jnp.float32)
    xf2 = x2.astype(jnp.float32)
    rot = jnp.concatenate([xf1 * cos - xf2 * sin, xf2 * cos + xf1 * sin], axis=-1).astype(x.dtype)
    return jnp.concatenate([x_nope, rot], axis=-1)


def _causal_block_attention(q, k, v):
    B, S, H, Dqk = q.shape
    nb = S // Q_BLOCK
    scale = Dqk ** -0.5
    q_blocks = q.reshape(B, nb, Q_BLOCK, H, Dqk).transpose(1, 0, 3, 2, 4)
    k_t = k.transpose(0, 2, 1, 3)
    v_t = v.transpose(0, 2, 1, 3)
    key_pos = jnp.arange(S)

    def attend(args):
        q_blk, blk = args
        s = jnp.einsum("bhqd,bhkd->bhqk", q_blk, k_t).astype(jnp.float32) * scale
        q_pos = blk * Q_BLOCK + jnp.arange(Q_BLOCK)
        s = jnp.where(key_pos[None, :] <= q_pos[:, None], s, -jnp.inf)
        p = jax.nn.softmax(s, axis=-1).astype(v_t.dtype)
        return jnp.einsum("bhqk,bhkv->bhqv", p, v_t)

    o = lax.map(attend, (q_blocks, jnp.arange(nb)))
    return o.transpose(1, 0, 3, 2, 4).reshape(B, S, H, -1)


def _mla_mixer(h, cos, sin, w_in, q_a_norm, w_q_up, kv_a_norm, w_kv_up, q_norm, k_norm, w_out):
    B, S, _ = h.shape
    c_q, c_kv, k_rope = jnp.split(h @ w_in, [MLA_Q_LORA, MLA_Q_LORA + MLA_KV_LORA], axis=-1)
    q = (_rmsnorm(c_q, q_a_norm) @ w_q_up).reshape(B, S, MLA_HEADS, MLA_QK_HEAD)
    kv = (_rmsnorm(c_kv, kv_a_norm) @ w_kv_up).reshape(B, S, MLA_HEADS, MLA_NOPE + MLA_V_HEAD)
    k_nope, v = jnp.split(kv, [MLA_NOPE], axis=-1)
    k = jnp.concatenate(
        [k_nope, jnp.broadcast_to(k_rope[:, :, None, :], (B, S, MLA_HEADS, MLA_ROPE))], axis=-1)
    q = _rope_tail(_rmsnorm(q, q_norm), cos, sin)
    k = _rope_tail(_rmsnorm(k, k_norm), cos, sin)
    o = _causal_block_attention(q, k, v)
    return o.reshape(B, S, MLA_HEADS * MLA_V_HEAD) @ w_out


def _conv_ffn(h, w_up, conv_w, conv_b, w_down):
    S = h.shape[1]
    u = h @ w_up
    u_pad = jnp.pad(u, ((0, 0), (FFN_CONV - 1, 0), (0, 0)))
    y = conv_b.astype(u.dtype)
    for j in range(FFN_CONV):
        y = y + u_pad[:, j:j + S, :] * conv_w[j]
    gate, up = jnp.split(y, 2, axis=-1)
    return (jax.nn.silu(gate) * up) @ w_down


def setup_inputs(seed: int = 0) -> dict:
    key = jax.random.key(seed)
    ks = iter(jax.random.split(key, 32))

    def nrm(shape, scale):
        return jax.random.normal(next(ks), shape, jnp.float32) * scale

    def gain(shape):
        return 1.0 + 0.02 * jax.random.normal(next(ks), shape, jnp.float32)

    D = D_MODEL
    x = jax.random.normal(next(ks), (BATCH, SEQ, D), jnp.float32)
    offsets = jax.random.randint(next(ks), (BATCH, 1), 0, 4096, dtype=jnp.int32)
    positions = offsets + jnp.arange(SEQ, dtype=jnp.int32)[None, :]
    return {
        "x": x,
        "positions": positions,
        "norm_mix": gain((DEPTH, D)),
        "norm_ffn": gain((DEPTH, D)),
        "hgrn_w_in": nrm((N_HGRN_LAYERS, D, 4 * D), D ** -0.5),
        "hgrn_lower_bounds": nrm((N_HGRN_LAYERS, HGRN_HEADS * HGRN_EXPAND), 0.1),
        "hgrn_out_norm": gain((N_HGRN_LAYERS, HGRN_VDIM)),
        "hgrn_w_out": nrm((N_HGRN_LAYERS, D, D), D ** -0.5),
        "mla_w_in": nrm((N_MLA_LAYERS, D, MLA_Q_LORA + MLA_KV_LORA + MLA_ROPE), D ** -0.5),
        "mla_q_a_norm": gain((N_MLA_LAYERS, MLA_Q_LORA)),
        "mla_w_q_up": nrm((N_MLA_LAYERS, MLA_Q_LORA, MLA_HEADS * MLA_QK_HEAD), MLA_Q_LORA ** -0.5),
        "mla_kv_a_norm": gain((N_MLA_LAYERS, MLA_KV_LORA)),
        "mla_w_kv_up": nrm((N_MLA_LAYERS, MLA_KV_LORA, MLA_HEADS * (MLA_NOPE + MLA_V_HEAD)), MLA_KV_LORA ** -0.5),
        "mla_q_norm": gain((N_MLA_LAYERS, MLA_QK_HEAD)),
        "mla_k_norm": gain((N_MLA_LAYERS, MLA_QK_HEAD)),
        "mla_w_out": nrm((N_MLA_LAYERS, MLA_HEADS * MLA_V_HEAD, D), (MLA_HEADS * MLA_V_HEAD) ** -0.5),
        "ffn_w_up": nrm((DEPTH, D, 2 * D_FF), D ** -0.5),
        "ffn_conv_w": nrm((DEPTH, FFN_CONV, 2 * D_FF), FFN_CONV ** -0.5),
        "ffn_conv_b": nrm((DEPTH, 2 * D_FF), 0.01),
        "ffn_w_down": nrm((DEPTH, D_FF, D), D_FF ** -0.5),
    }


def reference(x, positions, norm_mix, norm_ffn, hgrn_w_in, hgrn_lower_bounds, hgrn_out_norm, hgrn_w_out,
              mla_w_in, mla_q_a_norm, mla_w_q_up, mla_kv_a_norm, mla_w_kv_up, mla_q_norm, mla_k_norm,
              mla_w_out, ffn_w_up, ffn_conv_w, ffn_conv_b, ffn_w_down):
    lb_soft = jax.nn.softmax(hgrn_lower_bounds.astype(jnp.float32), axis=0)
    lower_bounds = jnp.cumsum(lb_soft, axis=0) - lb_soft[0:1]
    inv_freq = ROPE_THETA ** (-jnp.arange(0, MLA_ROPE, 2, dtype=jnp.float32) / MLA_ROPE)
    ang = positions.astype(jnp.float32)[..., None] * inv_freq
    cos = jnp.cos(ang)[:, :, None, :]
    sin = jnp.sin(ang)[:, :, None, :]

    for layer in range(DEPTH):
        h = _rmsnorm(x, norm_mix[layer])
        j = layer // N_MIXERS
        if layer % N_MIXERS == 0:
            y = _hgrn2_mixer(h, hgrn_w_in[j], lower_bounds[j], hgrn_out_norm[j], hgrn_w_out[j])
        else:
            y = _mla_mixer(h, cos, sin, mla_w_in[j], mla_q_a_norm[j], mla_w_q_up[j], mla_kv_a_norm[j],
                           mla_w_kv_up[j], mla_q_norm[j], mla_k_norm[j], mla_w_out[j])
        x = x + y
        h = _rmsnorm(x, norm_ffn[layer])
        x = x + _conv_ffn(h, ffn_w_up[layer], ffn_conv_w[layer], ffn_conv_b[layer], ffn_w_down[layer])
    return x
```

```python
import functools
import math

import jax
import jax.numpy as jnp
from jax import lax
from jax.experimental import pallas as pl
from jax.experimental.pallas import tpu as pltpu

F32 = jnp.float32
BF16 = jnp.bfloat16

RMS_EPS = 1e-6
ROPE_THETA = 10000.0

LANES = 128
SUBLANES = 8
VMEM_LIMIT_BYTES = 56 * 1024 * 1024

HGRN_HEAD = 128
HGRN_CHUNK = 64
HGRN_SUB = 16
MLA_NOPE = 128
MLA_ROPE = 64
MLA_QK = MLA_NOPE + MLA_ROPE
MLA_V = 128
MLA_QK_PAD = 2 * LANES
FFN_CONV = 3
NEG_BIG = -1e30

_NT = (((1,), (1,)), ((), ()))


def _compiler_params(semantics):
    return pltpu.CompilerParams(dimension_semantics=semantics, vmem_limit_bytes=VMEM_LIMIT_BYTES)


def _resident(shape, index_map):
    return pl.BlockSpec(shape, index_map, pipeline_mode=pl.Buffered(1))


def _rms_scale(x, n):
    return lax.rsqrt(jnp.sum(x * x, axis=-1, keepdims=True) * (1.0 / n) + RMS_EPS)


def _sigmoid(x):
    return 1.0 / (1.0 + jnp.exp(-x))


def _hgrn_in_kernel(x_ref, g_ref, w_ref, lb_ref, q_ref, lf_ref, k_ref, v_ref, gate_ref):
    x = x_ref[...]
    d = x.shape[-1]
    h = (x * _rms_scale(x, d) * g_ref[...]).astype(BF16)
    p = jnp.dot(h, w_ref[...], preferred_element_type=F32)
    q = p[:, 0 * d:1 * d]
    f = p[:, 1 * d:2 * d]
    lb = lb_ref[...]
    forget = lb + (1.0 - lb) * _sigmoid(f)
    q_ref[...] = q * _sigmoid(q)
    lf_ref[...] = jnp.log(forget)
    k_ref[...] = 1.0 - forget
    v_ref[...] = p[:, 2 * d:3 * d].astype(BF16)
    g = p[:, 3 * d:4 * d]
    gate_ref[...] = g * _sigmoid(g)


def _hgrn_in(x2d, gain, w_in, lb, tm):
    m, d = x2d.shape
    row = pl.BlockSpec((tm, d), lambda i: (i, 0))
    vec = _resident((1, d), lambda i: (0, 0))
    return pl.pallas_call(
        _hgrn_in_kernel,
        grid=(m // tm,),
        in_specs=[row, vec, _resident((d, 4 * d), lambda i: (0, 0)), vec],
        out_specs=[row, row, row, row, row],
        out_shape=[
            jax.ShapeDtypeStruct((m, d), F32),
            jax.ShapeDtypeStruct((m, d), F32),
            jax.ShapeDtypeStruct((m, d), F32),
            jax.ShapeDtypeStruct((m, d), BF16),
            jax.ShapeDtypeStruct((m, d), F32),
        ],
        compiler_params=_compiler_params(("arbitrary",)),
        name="hgrn_in",
    )(x2d, gain, w_in, lb)


def _cumsum_rows(x):
    c = x.shape[0]
    row = lax.broadcasted_iota(jnp.int32, (c, c), 0)
    col = lax.broadcasted_iota(jnp.int32, (c, c), 1)
    tri = jnp.where(col <= row, 1.0, 0.0).astype(BF16)
    hi = x.astype(BF16)
    rem = x - hi.astype(F32)
    mid = rem.astype(BF16)
    lo = (rem - mid.astype(F32)).astype(BF16)
    out = jnp.dot(tri, hi, preferred_element_type=F32)
    out = out + jnp.dot(tri, mid, preferred_element_type=F32)
    return out + jnp.dot(tri, lo, preferred_element_type=F32)


def _hgrn_chunk(q, lf, k, v, gate, gain, state_ref):
    c = HGRN_CHUNK
    sub = HGRN_SUB
    half = SUBLANES
    g_cum = _cumsum_rows(lf)
    g_last = g_cum[c - 1:c, :]

    state_t = state_ref[...]
    q_dec = (q * jnp.exp(g_cum)).astype(BF16)
    o = lax.dot_general(q_dec, state_t.astype(BF16), _NT, preferred_element_type=F32)

    k_dec = (k * jnp.exp(g_last - g_cum)).astype(BF16)
    v_t = v.astype(F32).T.astype(BF16)
    state_ref[...] = state_t * jnp.exp(g_last) + jnp.dot(v_t, k_dec, preferred_element_type=F32)

    row_c = lax.broadcasted_iota(jnp.int32, (c, HGRN_HEAD), 0)
    lane = lax.broadcasted_iota(jnp.int32, (half, c), 1)
    sub_row = lax.broadcasted_iota(jnp.int32, (half, HGRN_HEAD), 0)
    blocks = []
    for j in range(c // sub):
        r0 = j * sub
        qb = q[r0:r0 + sub]
        kb = k[r0:r0 + sub]
        gb = g_cum[r0:r0 + sub]
        if j == 0:
            a_top = jnp.zeros((half, c), F32)
            a_bot = jnp.zeros((half, c), F32)
        else:
            g_ref = g_cum[r0 - 1:r0, :]
            q_off = (qb * jnp.exp(gb - g_ref)).astype(BF16)
            k_off = k * jnp.exp(jnp.minimum(g_ref - g_cum, 0.0))
            k_off = jnp.where(row_c < r0, k_off, 0.0).astype(BF16)
            a_off = lax.dot_general(q_off, k_off, _NT, preferred_element_type=F32)
            a_top = a_off[:half]
            a_bot = a_off[half:]
        q_top, q_bot = qb[:half], qb[half:]
        g_top, g_bot = gb[:half], gb[half:]
        for s in range(sub):
            gs = gb[s:s + 1]
            ks = kb[s:s + 1]
            if s < half:
                e_top = jnp.exp(jnp.where(sub_row >= s, g_top - gs, NEG_BIG))
                col_top = jnp.sum(q_top * (ks * e_top), axis=-1, keepdims=True)
                a_top = jnp.where(lane == r0 + s, col_top, a_top)
                e_bot = jnp.exp(g_bot - gs)
            else:
                e_bot = jnp.exp(jnp.where(sub_row >= s - half, g_bot - gs, NEG_BIG))
            col_bot = jnp.sum(q_bot * (ks * e_bot), axis=-1, keepdims=True)
            a_bot = jnp.where(lane == r0 + s, col_bot, a_bot)
        blocks += [a_top, a_bot]
    a = jnp.concatenate(blocks, axis=0).astype(BF16)
    o = o + jnp.dot(a, v, preferred_element_type=F32)

    y = o * _rms_scale(o, HGRN_HEAD) * gain
    return (y * gate).astype(BF16)


def _hgrn_rec_kernel(q_ref, lf_ref, k_ref, v_ref, gate_ref, gain_ref, o_ref, state_ref):
    @pl.when(pl.program_id(2) == 0)
    def _():
        state_ref[...] = jnp.zeros_like(state_ref)

    gain = gain_ref[...]
    n_chunks = q_ref.shape[1] // HGRN_CHUNK

    def body(ci, carry):
        rows = pl.ds(pl.multiple_of(ci * HGRN_CHUNK, HGRN_CHUNK), HGRN_CHUNK)
        o_ref[0, rows, :] = _hgrn_chunk(
            q_ref[0, rows, :], lf_ref[0, rows, :], k_ref[0, rows, :], v_ref[0, rows, :],
            gate_ref[0, rows, :], gain, state_ref)
        return carry

    lax.fori_loop(0, n_chunks, body, 0)


def _hgrn_rec(q, lf, k, v, gate, gain, tt):
    b, s, d = q.shape
    heads = d // HGRN_HEAD
    blk = pl.BlockSpec((1, tt, HGRN_HEAD), lambda bi, hi, ti: (bi, ti, hi))
    return pl.pallas_call(
        _hgrn_rec_kernel,
        grid=(b, heads, s // tt),
        in_specs=[blk, blk, blk, blk, blk, pl.BlockSpec((1, HGRN_HEAD), lambda bi, hi, ti: (0, 0))],
        out_specs=blk,
        out_shape=jax.ShapeDtypeStruct((b, s, d), BF16),
        scratch_shapes=[pltpu.VMEM((HGRN_HEAD, HGRN_HEAD), F32)],
        compiler_params=_compiler_params(("arbitrary", "arbitrary", "arbitrary")),
        name="hgrn_rec",
    )(q, lf, k, v, gate, gain)


def _matmul_residual_kernel(x_ref, a_ref, w_ref, o_ref):
    o_ref[...] = x_ref[...] + jnp.dot(a_ref[...], w_ref[...], preferred_element_type=F32)


def _matmul_residual(x2d, a2d, w, tm):
    m, d = x2d.shape
    kdim = a2d.shape[1]
    return pl.pallas_call(
        _matmul_residual_kernel,
        grid=(m // tm,),
        in_specs=[
            pl.BlockSpec((tm, d), lambda i: (i, 0)),
            pl.BlockSpec((tm, kdim), lambda i: (i, 0)),
            _resident((kdim, d), lambda i: (0, 0)),
        ],
        out_specs=pl.BlockSpec((tm, d), lambda i: (i, 0)),
        out_shape=jax.ShapeDtypeStruct((m, d), F32),
        compiler_params=_compiler_params(("arbitrary",)),
        name="matmul_residual",
    )(x2d, a2d, w)


def _rope(t, cos_t, sin_t):
    return t * cos_t + pltpu.roll(t, LANES // 2, axis=1) * sin_t


def _mla_in_kernel(x_ref, cos_ref, sin_ref, g_ref, w_in_ref, qa_ref, w_q_ref, kva_ref, w_kv_ref,
                   qn_ref, kn_ref, q_out, k_out, v_out, *, q_scale, q_lora, kv_lora):
    x = x_ref[0]
    d = x.shape[-1]
    h = (x * _rms_scale(x, d) * g_ref[...]).astype(BF16)
    c = jnp.dot(h, w_in_ref[...], preferred_element_type=F32)
    c_q = c[:, :q_lora]
    c_kv = c[:, q_lora:q_lora + kv_lora]
    k_rope = c[:, q_lora + kv_lora:]
    c_q = (c_q * _rms_scale(c_q, q_lora) * qa_ref[...]).astype(BF16)
    c_kv = (c_kv * _rms_scale(c_kv, kv_lora) * kva_ref[...]).astype(BF16)
    q = jnp.dot(c_q, w_q_ref[...], preferred_element_type=F32)
    kv = jnp.dot(c_kv, w_kv_ref[...], preferred_element_type=F32)

    cos_t = cos_ref[0]
    sin_t = sin_ref[0]
    qn = qn_ref[...]
    kn = kn_ref[...]
    qn_nope, qn_rope = qn[:, :LANES], qn[:, LANES:]
    kn_nope, kn_rope = kn[:, :LANES], kn[:, LANES:]
    k_rope_ss = jnp.sum(k_rope * k_rope, axis=-1, keepdims=True)
    heads = q_out.shape[1]
    for hd in range(heads):
        base = hd * MLA_QK_PAD
        q_nope = q[:, base:base + LANES]
        q_rope = q[:, base + LANES:base + 2 * LANES]
        ss = jnp.sum(q_nope * q_nope, axis=-1, keepdims=True) + jnp.sum(q_rope * q_rope, axis=-1, keepdims=True)
        r = lax.rsqrt(ss * (1.0 / MLA_QK) + RMS_EPS)
        q_out[0, hd, :, :LANES] = (q_nope * r * qn_nope * q_scale).astype(BF16)
        q_out[0, hd, :, LANES:] = (_rope(q_rope * r * qn_rope, cos_t, sin_t) * q_scale).astype(BF16)

        k_nope = kv[:, base:base + LANES]
        ssk = jnp.sum(k_nope * k_nope, axis=-1, keepdims=True) + k_rope_ss
        rk = lax.rsqrt(ssk * (1.0 / MLA_QK) + RMS_EPS)
        k_out[0, hd, :, :LANES] = (k_nope * rk * kn_nope).astype(BF16)
        k_out[0, hd, :, LANES:] = _rope(k_rope * rk * kn_rope, cos_t, sin_t).astype(BF16)
        v_out[0, hd] = kv[:, base + LANES:base + 2 * LANES].astype(BF16)


def _mla_in(x, cos_t, sin_t, gain, w_in, qa, w_q, kva, w_kv, qn, kn, heads, tm):
    b, s, d = x.shape
    q_lora = qa.shape[1]
    kv_lora = kva.shape[1]
    q_scale = (MLA_QK ** -0.5) * math.log2(math.e)
    const = lambda bi, si: (0, 0)
    row = lambda w: pl.BlockSpec((1, tm, w), lambda bi, si: (bi, si, 0))
    head_out = lambda w: pl.BlockSpec((1, heads, tm, w), lambda bi, si: (bi, 0, si, 0))
    return pl.pallas_call(
        functools.partial(_mla_in_kernel, q_scale=q_scale, q_lora=q_lora, kv_lora=kv_lora),
        grid=(b, s // tm),
        in_specs=[
            row(d), row(LANES), row(LANES),
            _resident((1, d), const),
            _resident(w_in.shape, const),
            _resident(qa.shape, const),
            _resident(w_q.shape, const),
            _resident(kva.shape, const),
            _resident(w_kv.shape, const),
            _resident(qn.shape, const),
            _resident(kn.shape, const),
        ],
        out_specs=[head_out(MLA_QK_PAD), head_out(MLA_QK_PAD), head_out(MLA_V)],
        out_shape=[
            jax.ShapeDtypeStruct((b, heads, s, MLA_QK_PAD), BF16),
            jax.ShapeDtypeStruct((b, heads, s, MLA_QK_PAD), BF16),
            jax.ShapeDtypeStruct((b, heads, s, MLA_V), BF16),
        ],
        compiler_params=_compiler_params(("arbitrary", "arbitrary")),
        name="mla_in",
    )(x, cos_t, sin_t, gain, w_in, qa, w_q, kva, w_kv, qn, kn)


def _attn_kernel(q_ref, k_ref, v_ref, o_ref, acc_ref, *, tq):
    qi = pl.program_id(2)
    q = q_ref[0, 0]

    def block(ki, m, l, masked):
        rows = pl.ds(pl.multiple_of(ki * tq, tq), tq)
        s = lax.dot_general(q, k_ref[0, 0, rows, :], _NT, preferred_element_type=F32)
        if masked:
            r = lax.broadcasted_iota(jnp.int32, s.shape, 0)
            c = lax.broadcasted_iota(jnp.int32, s.shape, 1)
            s = jnp.where(c <= r, s, NEG_BIG)
        m_new = jnp.maximum(m, jnp.max(s, axis=-1, keepdims=True))
        alpha = jnp.exp2(m - m_new)
        p = jnp.exp2(s - m_new)
        l_new = alpha * l + jnp.sum(p, axis=-1, keepdims=True)
        pv = jnp.dot(p.astype(BF16), v_ref[0, 0, rows, :], preferred_element_type=F32)
        acc_ref[...] = alpha * acc_ref[...] + pv
        return m_new, l_new

    acc_ref[...] = jnp.zeros_like(acc_ref)
    m0 = jnp.full((tq, 1), NEG_BIG, F32)
    l0 = jnp.zeros((tq, 1), F32)
    m, l = lax.fori_loop(0, qi, lambda ki, ml: block(ki, ml[0], ml[1], False), (m0, l0))
    m, l = block(qi, m, l, True)
    o_ref[0] = (acc_ref[...] / l).astype(o_ref.dtype)


def _attention(q, k, v, tq):
    b, heads, s, dq = q.shape
    dv = v.shape[-1]
    return pl.pallas_call(
        functools.partial(_attn_kernel, tq=tq),
        grid=(b, heads, s // tq),
        in_specs=[
            pl.BlockSpec((1, 1, tq, dq), lambda bi, hi, qi: (bi, hi, qi, 0)),
            pl.BlockSpec((1, 1, s, dq), lambda bi, hi, qi: (bi, hi, 0, 0)),
            pl.BlockSpec((1, 1, s, dv), lambda bi, hi, qi: (bi, hi, 0, 0)),
        ],
        out_specs=pl.BlockSpec((1, tq, dv), lambda bi, hi, qi: (bi, qi, hi)),
        out_shape=jax.ShapeDtypeStruct((b, s, heads * dv), BF16),
        scratch_shapes=[pltpu.VMEM((tq, dv), F32)],
        compiler_params=_compiler_params(("arbitrary", "arbitrary", "arbitrary")),
        name="mla_attention",
    )(q, k, v)


def _ffn_kernel(x_ref, g_ref, w_up_ref, cw_ref, cb_ref, w_down_ref, o_ref, tail_ref, shift_ref,
                *, d_ff, fc):
    @pl.when(pl.program_id(1) == 0)
    def _():
        tail_ref[...] = jnp.zeros_like(tail_ref)

    x = x_ref[0]
    tm, d = x.shape
    h = (x * _rms_scale(x, d) * g_ref[...]).astype(BF16)

    def conv(u, col0):
        cols = slice(col0, col0 + fc)
        shift_ref[SUBLANES:SUBLANES + tm, :] = u
        shift_ref[0:SUBLANES, :] = tail_ref[:, cols]
        u1 = shift_ref[SUBLANES - 1:SUBLANES - 1 + tm, :]
        u2 = shift_ref[SUBLANES - 2:SUBLANES - 2 + tm, :]
        tail_ref[:, cols] = u[tm - SUBLANES:tm, :]
        w = cw_ref[:, cols]
        return cb_ref[:, cols] + u2 * w[0:1] + u1 * w[1:2] + u * w[2:3]

    acc = jnp.zeros((tm, d), F32)
    for j in range(d_ff // fc):
        c0 = j * fc
        u_gate = jnp.dot(h, w_up_ref[:, c0:c0 + fc], preferred_element_type=F32)
        gate = conv(u_gate, c0)
        u_up = jnp.dot(h, w_up_ref[:, d_ff + c0:d_ff + c0 + fc], preferred_element_type=F32)
        up = conv(u_up, d_ff + c0)
        act = (gate * _sigmoid(gate) * up).astype(BF16)
        acc = acc + jnp.dot(act, w_down_ref[c0:c0 + fc, :], preferred_element_type=F32)
    o_ref[0] = x + acc


def _ffn(x, gain, w_up, conv_w, conv_b, w_down, tm, fc):
    b, s, d = x.shape
    d_ff = w_down.shape[0]
    const = lambda bi, si: (0, 0)
    row = pl.BlockSpec((1, tm, d), lambda bi, si: (bi, si, 0))
    return pl.pallas_call(
        functools.partial(_ffn_kernel, d_ff=d_ff, fc=fc),
        grid=(b, s // tm),
        in_specs=[
            row,
            _resident((1, d), const),
            _resident(w_up.shape, const),
            _resident(conv_w.shape, const),
            _resident(conv_b.shape, const),
            _resident(w_down.shape, const),
        ],
        out_specs=row,
        out_shape=jax.ShapeDtypeStruct((b, s, d), F32),
        scratch_shapes=[
            pltpu.VMEM((SUBLANES, 2 * d_ff), F32),
            pltpu.VMEM((SUBLANES + tm, fc), F32),
        ],
        compiler_params=_compiler_params(("arbitrary", "arbitrary")),
        name="conv_ffn",
    )(x, gain, w_up, conv_w, conv_b, w_down)


def _spread_rope(w):
    half = MLA_ROPE // 2
    z = jnp.zeros(w.shape[:-1] + (LANES // 2 - half,), w.dtype)
    return jnp.concatenate([w[..., :half], z, w[..., half:], z], axis=-1)


def _pad_qk_channels(w, heads):
    w = w.reshape(w.shape[:-1] + (heads, MLA_QK))
    w = jnp.concatenate([w[..., :MLA_NOPE], _spread_rope(w[..., MLA_NOPE:])], axis=-1)
    return w.reshape(w.shape[:-2] + (heads * MLA_QK_PAD,))


def kernel(x, positions, norm_mix, norm_ffn, hgrn_w_in, hgrn_lower_bounds, hgrn_out_norm, hgrn_w_out,
           mla_w_in, mla_q_a_norm, mla_w_q_up, mla_kv_a_norm, mla_w_kv_up, mla_q_norm, mla_k_norm,
           mla_w_out, ffn_w_up, ffn_conv_w, ffn_conv_b, ffn_w_down):
    b, s, d = x.shape
    depth = norm_mix.shape[0]
    n_mixers = 2
    q_lora = mla_q_a_norm.shape[1]
    kv_lora = mla_kv_a_norm.shape[1]
    heads = mla_w_q_up.shape[2] // MLA_QK

    lb_soft = jax.nn.softmax(hgrn_lower_bounds.astype(F32), axis=0)
    lower_bounds = jnp.cumsum(lb_soft, axis=0) - lb_soft[0:1]
    inv_freq = ROPE_THETA ** (-jnp.arange(0, MLA_ROPE, 2, dtype=F32) / MLA_ROPE)
    ang = positions.astype(F32)[..., None] * inv_freq
    cos, sin = jnp.cos(ang), jnp.sin(ang)
    zeros = jnp.zeros_like(cos)
    cos_t = jnp.concatenate([cos, zeros, cos, zeros], axis=-1)
    sin_t = jnp.concatenate([-sin, zeros, sin, zeros], axis=-1)

    x = x.astype(F32)
    for layer in range(depth):
        j = layer // n_mixers
        gain = norm_mix[layer].reshape(1, d).astype(F32)
        if layer % n_mixers == 0:
            q, lf, k, v, gate = _hgrn_in(
                x.reshape(b * s, d), gain, hgrn_w_in[j].astype(BF16),
                lower_bounds[j].reshape(1, d), tm=256)
            shp = (b, s, d)
            o = _hgrn_rec(q.reshape(shp), lf.reshape(shp), k.reshape(shp), v.reshape(shp),
                          gate.reshape(shp), hgrn_out_norm[j].reshape(1, HGRN_HEAD).astype(F32), tt=512)
            w_out = hgrn_w_out[j]
        else:
            w_in = jnp.concatenate(
                [mla_w_in[j][:, :q_lora + kv_lora], _spread_rope(mla_w_in[j][:, q_lora + kv_lora:])], axis=-1)
            qh, kh, vh = _mla_in(
                x, cos_t, sin_t, gain, w_in.astype(BF16),
                mla_q_a_norm[j].reshape(1, q_lora).astype(F32),
                _pad_qk_channels(mla_w_q_up[j], heads).astype(BF16),
                mla_kv_a_norm[j].reshape(1, kv_lora).astype(F32),
                mla_w_kv_up[j].astype(BF16),
                _pad_qk_channels(mla_q_norm[j].reshape(1, MLA_QK), 1).astype(F32),
                _pad_qk_channels(mla_k_norm[j].reshape(1, MLA_QK), 1).astype(F32),
                heads=heads, tm=256)
            o = _attention(qh, kh, vh, tq=512)
            w_out = mla_w_out[j]
        x = _matmul_residual(x.reshape(b * s, d), o.reshape(b * s, d), w_out.astype(BF16), tm=512)
        x = _ffn(x.reshape(b, s, d), norm_ffn[layer].reshape(1, d).astype(F32),
                 ffn_w_up[layer].astype(BF16), ffn_conv_w[layer].astype(F32),
                 ffn_conv_b[layer].reshape(1, -1).astype(F32), ffn_w_down[layer].astype(BF16),
                 tm=512, fc=256)
    return x
```

```python
import functools
import math

import jax
import jax.numpy as jnp
from jax import lax
from jax.experimental import pallas as pl
from jax.experimental.pallas import tpu as pltpu

F32 = jnp.float32
BF16 = jnp.bfloat16

RMS_EPS = 1e-6
ROPE_THETA = 10000.0

LANES = 128
SUBLANES = 8
VMEM_LIMIT_BYTES = 56 * 1024 * 1024

HGRN_HEAD = 128
HGRN_CHUNK = 64
HGRN_SUB = 16
MLA_NOPE = 128
MLA_ROPE = 64
MLA_QK = MLA_NOPE + MLA_ROPE
MLA_V = 128
MLA_QK_PAD = 2 * LANES
FFN_CONV = 3
ATTN_ROW_GROUP = 64
NEG_BIG = -1e30

_NT = (((1,), (1,)), ((), ()))


def _compiler_params(semantics):
    return pltpu.CompilerParams(dimension_semantics=semantics, vmem_limit_bytes=VMEM_LIMIT_BYTES)


def _resident(shape, index_map):
    return pl.BlockSpec(shape, index_map, pipeline_mode=pl.Buffered(1))


def _rms_scale(x, n):
    return lax.rsqrt(jnp.sum(x * x, axis=-1, keepdims=True) * (1.0 / n) + RMS_EPS)


def _sigmoid(x):
    return 1.0 / (1.0 + jnp.exp(-x))


def _hgrn_in_kernel(x_ref, g_ref, w_ref, lb_ref, q_ref, lf_ref, k_ref, v_ref, gate_ref):
    x = x_ref[...]
    d = x.shape[-1]
    h = (x * _rms_scale(x, d) * g_ref[...]).astype(BF16)
    p = jnp.dot(h, w_ref[...], preferred_element_type=F32)
    q = p[:, 0 * d:1 * d]
    f = p[:, 1 * d:2 * d]
    lb = lb_ref[...]
    forget = lb + (1.0 - lb) * _sigmoid(f)
    q_ref[...] = q * _sigmoid(q)
    lf_ref[...] = jnp.log(forget)
    k_ref[...] = 1.0 - forget
    v_ref[...] = p[:, 2 * d:3 * d].astype(BF16)
    g = p[:, 3 * d:4 * d]
    gate_ref[...] = g * _sigmoid(g)


def _hgrn_in(x2d, gain, w_in, lb, tm):
    m, d = x2d.shape
    row = pl.BlockSpec((tm, d), lambda i: (i, 0))
    vec = _resident((1, d), lambda i: (0, 0))
    return pl.pallas_call(
        _hgrn_in_kernel,
        grid=(m // tm,),
        in_specs=[row, vec, _resident((d, 4 * d), lambda i: (0, 0)), vec],
        out_specs=[row, row, row, row, row],
        out_shape=[
            jax.ShapeDtypeStruct((m, d), F32),
            jax.ShapeDtypeStruct((m, d), F32),
            jax.ShapeDtypeStruct((m, d), F32),
            jax.ShapeDtypeStruct((m, d), BF16),
            jax.ShapeDtypeStruct((m, d), F32),
        ],
        compiler_params=_compiler_params(("arbitrary",)),
        name="hgrn_in",
    )(x2d, gain, w_in, lb)


def _cumsum_rows(x):
    c = x.shape[0]
    row = lax.broadcasted_iota(jnp.int32, (c, c), 0)
    col = lax.broadcasted_iota(jnp.int32, (c, c), 1)
    tri = jnp.where(col <= row, 1.0, 0.0).astype(BF16)
    hi = x.astype(BF16)
    rem = x - hi.astype(F32)
    mid = rem.astype(BF16)
    lo = (rem - mid.astype(F32)).astype(BF16)
    out = jnp.dot(tri, hi, preferred_element_type=F32)
    out = out + jnp.dot(tri, mid, preferred_element_type=F32)
    return out + jnp.dot(tri, lo, preferred_element_type=F32)


def _hgrn_chunk(q, lf, k, v, gate, gain, state_t):
    c = HGRN_CHUNK
    sub = HGRN_SUB
    half = SUBLANES
    g_cum = _cumsum_rows(lf)
    g_last = g_cum[c - 1:c, :]

    q_dec = (q * jnp.exp(g_cum)).astype(BF16)
    o = lax.dot_general(q_dec, state_t.astype(BF16), _NT, preferred_element_type=F32)

    k_dec = (k * jnp.exp(g_last - g_cum)).astype(BF16)
    v_t = v.astype(F32).T.astype(BF16)
    new_state_t = state_t * jnp.exp(g_last) + jnp.dot(v_t, k_dec, preferred_element_type=F32)

    row_c = lax.broadcasted_iota(jnp.int32, (c, HGRN_HEAD), 0)
    lane = lax.broadcasted_iota(jnp.int32, (half, c), 1)
    sub_row = lax.broadcasted_iota(jnp.int32, (half, HGRN_HEAD), 0)
    blocks = []
    for j in range(c // sub):
        r0 = j * sub
        qb = q[r0:r0 + sub]
        kb = k[r0:r0 + sub]
        gb = g_cum[r0:r0 + sub]
        if j == 0:
            a_top = jnp.zeros((half, c), F32)
            a_bot = jnp.zeros((half, c), F32)
        else:
            g_ref = g_cum[r0 - 1:r0, :]
            q_off = (qb * jnp.exp(gb - g_ref)).astype(BF16)
            k_off = k * jnp.exp(jnp.minimum(g_ref - g_cum, 0.0))
            k_off = jnp.where(row_c < r0, k_off, 0.0).astype(BF16)
            a_off = lax.dot_general(q_off, k_off, _NT, preferred_element_type=F32)
            a_top = a_off[:half]
            a_bot = a_off[half:]
        q_top, q_bot = qb[:half], qb[half:]
        g_top, g_bot = gb[:half], gb[half:]
        for s in range(sub):
            gs = gb[s:s + 1]
            ks = kb[s:s + 1]
            if s < half:
                e_top = jnp.exp(jnp.where(sub_row >= s, g_top - gs, NEG_BIG))
                col_top = jnp.sum(q_top * (ks * e_top), axis=-1, keepdims=True)
                a_top = jnp.where(lane == r0 + s, col_top, a_top)
                e_bot = jnp.exp(g_bot - gs)
            else:
                e_bot = jnp.exp(jnp.where(sub_row >= s - half, g_bot - gs, NEG_BIG))
            col_bot = jnp.sum(q_bot * (ks * e_bot), axis=-1, keepdims=True)
            a_bot = jnp.where(lane == r0 + s, col_bot, a_bot)
        blocks += [a_top, a_bot]
    a = jnp.concatenate(blocks, axis=0).astype(BF16)
    o = o + jnp.dot(a, v, preferred_element_type=F32)

    y = o * _rms_scale(o, HGRN_HEAD) * gain
    return (y * gate).astype(BF16), new_state_t


def _hgrn_rec_kernel(q_ref, lf_ref, k_ref, v_ref, gate_ref, gain_ref, o_ref, state_ref):
    @pl.when(pl.program_id(2) == 0)
    def _():
        state_ref[...] = jnp.zeros_like(state_ref)

    gain = gain_ref[...]
    state_t = state_ref[...]
    for ci in range(q_ref.shape[1] // HGRN_CHUNK):
        rows = slice(ci * HGRN_CHUNK, (ci + 1) * HGRN_CHUNK)
        o_ref[0, rows, :], state_t = _hgrn_chunk(
            q_ref[0, rows, :], lf_ref[0, rows, :], k_ref[0, rows, :], v_ref[0, rows, :],
            gate_ref[0, rows, :], gain, state_t)
    state_ref[...] = state_t


def _hgrn_rec(q, lf, k, v, gate, gain, tt):
    b, s, d = q.shape
    heads = d // HGRN_HEAD
    blk = pl.BlockSpec((1, tt, HGRN_HEAD), lambda bi, hi, ti: (bi, ti, hi))
    return pl.pallas_call(
        _hgrn_rec_kernel,
        grid=(b, heads, s // tt),
        in_specs=[blk, blk, blk, blk, blk, pl.BlockSpec((1, HGRN_HEAD), lambda bi, hi, ti: (0, 0))],
        out_specs=blk,
        out_shape=jax.ShapeDtypeStruct((b, s, d), BF16),
        scratch_shapes=[pltpu.VMEM((HGRN_HEAD, HGRN_HEAD), F32)],
        compiler_params=_compiler_params(("arbitrary", "arbitrary", "arbitrary")),
        name="hgrn_rec",
    )(q, lf, k, v, gate, gain)


def _matmul_residual_kernel(x_ref, a_ref, w_ref, o_ref):
    o_ref[...] = x_ref[...] + jnp.dot(a_ref[...], w_ref[...], preferred_element_type=F32)


def _matmul_residual(x2d, a2d, w, tm):
    m, d = x2d.shape
    kdim = a2d.shape[1]
    return pl.pallas_call(
        _matmul_residual_kernel,
        grid=(m // tm,),
        in_specs=[
            pl.BlockSpec((tm, d), lambda i: (i, 0)),
            pl.BlockSpec((tm, kdim), lambda i: (i, 0)),
            _resident((kdim, d), lambda i: (0, 0)),
        ],
        out_specs=pl.BlockSpec((tm, d), lambda i: (i, 0)),
        out_shape=jax.ShapeDtypeStruct((m, d), F32),
        compiler_params=_compiler_params(("arbitrary",)),
        name="matmul_residual",
    )(x2d, a2d, w)


def _rope(t, cos_t, sin_t):
    return t * cos_t + pltpu.roll(t, LANES // 2, axis=1) * sin_t


def _mla_in_kernel(x_ref, cos_ref, sin_ref, g_ref, w_in_ref, qa_ref, w_q_ref, kva_ref, w_kv_ref,
                   qn_ref, kn_ref, q_out, kt_out, v_out, *, q_scale, q_lora, kv_lora):
    x = x_ref[0]
    d = x.shape[-1]
    h = (x * _rms_scale(x, d) * g_ref[...]).astype(BF16)
    c = jnp.dot(h, w_in_ref[...], preferred_element_type=F32)
    c_q = c[:, :q_lora]
    c_kv = c[:, q_lora:q_lora + kv_lora]
    k_rope = c[:, q_lora + kv_lora:]
    c_q = (c_q * _rms_scale(c_q, q_lora) * qa_ref[...]).astype(BF16)
    c_kv = (c_kv * _rms_scale(c_kv, kv_lora) * kva_ref[...]).astype(BF16)
    q = jnp.dot(c_q, w_q_ref[...], preferred_element_type=F32)
    kv = jnp.dot(c_kv, w_kv_ref[...], preferred_element_type=F32)

    cos_t = cos_ref[0]
    sin_t = sin_ref[0]
    qn = qn_ref[...]
    kn = kn_ref[...]
    qn_nope, qn_rope = qn[:, :LANES], qn[:, LANES:]
    kn_nope, kn_rope = kn[:, :LANES], kn[:, LANES:]
    k_rope_ss = jnp.sum(k_rope * k_rope, axis=-1, keepdims=True)
    heads = q_out.shape[1]
    for hd in range(heads):
        base = hd * MLA_QK_PAD
        q_nope = q[:, base:base + LANES]
        q_rope = q[:, base + LANES:base + 2 * LANES]
        ss = jnp.sum(q_nope * q_nope, axis=-1, keepdims=True) + jnp.sum(q_rope * q_rope, axis=-1, keepdims=True)
        r = lax.rsqrt(ss * (1.0 / MLA_QK) + RMS_EPS)
        q_out[0, hd, :, :LANES] = (q_nope * r * qn_nope * q_scale).astype(BF16)
        q_out[0, hd, :, LANES:] = (_rope(q_rope * r * qn_rope, cos_t, sin_t) * q_scale).astype(BF16)

        k_nope = kv[:, base:base + LANES]
        ssk = jnp.sum(k_nope * k_nope, axis=-1, keepdims=True) + k_rope_ss
        rk = lax.rsqrt(ssk * (1.0 / MLA_QK) + RMS_EPS)
        kt_out[0, hd, :LANES, :] = (k_nope * rk * kn_nope).T.astype(BF16)
        kt_out[0, hd, LANES:, :] = _rope(k_rope * rk * kn_rope, cos_t, sin_t).T.astype(BF16)
        v_out[0, hd, :, :MLA_V] = kv[:, base + LANES:base + 2 * LANES].astype(BF16)
        v_out[0, hd, :, MLA_V:] = jnp.ones((x.shape[0], LANES), BF16)


def _mla_in(x, cos_t, sin_t, gain, w_in, qa, w_q, kva, w_kv, qn, kn, heads, tm):
    b, s, d = x.shape
    q_lora = qa.shape[1]
    kv_lora = kva.shape[1]
    q_scale = (MLA_QK ** -0.5) * math.log2(math.e)
    const = lambda bi, si: (0, 0)
    row = lambda w: pl.BlockSpec((1, tm, w), lambda bi, si: (bi, si, 0))
    head_out = lambda w: pl.BlockSpec((1, heads, tm, w), lambda bi, si: (bi, 0, si, 0))
    return pl.pallas_call(
        functools.partial(_mla_in_kernel, q_scale=q_scale, q_lora=q_lora, kv_lora=kv_lora),
        grid=(b, s // tm),
        in_specs=[
            row(d), row(LANES), row(LANES),
            _resident((1, d), const),
            _resident(w_in.shape, const),
            _resident(qa.shape, const),
            _resident(w_q.shape, const),
            _resident(kva.shape, const),
            _resident(w_kv.shape, const),
            _resident(qn.shape, const),
            _resident(kn.shape, const),
        ],
        out_specs=[
            head_out(MLA_QK_PAD),
            pl.BlockSpec((1, heads, MLA_QK_PAD, tm), lambda bi, si: (bi, 0, 0, si)),
            head_out(MLA_V + LANES),
        ],
        out_shape=[
            jax.ShapeDtypeStruct((b, heads, s, MLA_QK_PAD), BF16),
            jax.ShapeDtypeStruct((b, heads, MLA_QK_PAD, s), BF16),
            jax.ShapeDtypeStruct((b, heads, s, MLA_V + LANES), BF16),
        ],
        compiler_params=_compiler_params(("arbitrary", "arbitrary")),
        name="mla_in",
    )(x, cos_t, sin_t, gain, w_in, qa, w_q, kva, w_kv, qn, kn)


def _attn_kernel(q_ref, kt_ref, v_ref, o_ref, acc_ref, m_ref, alpha_ref, s0_ref, s1_ref, p_ref, *, tq):
    qi = pl.program_id(2)
    hg = q_ref.shape[1]
    dv = v_ref.shape[-1] - LANES
    lane_tiles = tq // LANES

    def scores(ki, s_ref):
        cols = pl.ds(pl.multiple_of(ki * tq, tq), tq)
        for hd in range(hg):
            s_ref[hd] = jnp.dot(q_ref[0, hd], kt_ref[0, hd, :, cols], preferred_element_type=F32)

    def consume(ki, s_ref, masked):
        rows = pl.ds(pl.multiple_of(ki * tq, tq), tq)
        for hd in range(hg):
            for g in range(tq // ATTN_ROW_GROUP):
                rg = slice(g * ATTN_ROW_GROUP, (g + 1) * ATTN_ROW_GROUP)
                s = s_ref[hd, rg, :]
                if masked:
                    r = lax.broadcasted_iota(jnp.int32, s.shape, 0) + g * ATTN_ROW_GROUP
                    c = lax.broadcasted_iota(jnp.int32, s.shape, 1)
                    s = jnp.where(c <= r, s, NEG_BIG)
                m_prev = m_ref[hd, rg, :]
                m_new = jnp.maximum(m_prev, jnp.max(s, axis=-1, keepdims=True))
                alpha = jnp.exp2(m_prev - m_new)
                m_ref[hd, rg, :] = m_new
                alpha_ref[hd, rg, :LANES] = alpha
                alpha_ref[hd, rg, LANES:] = alpha
                p_ref[hd, rg, :] = jnp.exp2(s - pltpu.repeat(m_new, lane_tiles, axis=1)).astype(BF16)
            pv = jnp.dot(p_ref[hd], v_ref[0, hd, rows, :], preferred_element_type=F32)
            acc_ref[hd] = alpha_ref[hd] * acc_ref[hd] + pv

    def pair(kp, carry):
        k0 = 2 * kp
        scores(k0 + 1, s1_ref)
        consume(k0, s0_ref, False)
        scores(k0 + 2, s0_ref)
        consume(k0 + 1, s1_ref, False)
        return carry

    acc_ref[...] = jnp.zeros_like(acc_ref)
    m_ref[...] = jnp.full(m_ref.shape, NEG_BIG, F32)
    scores(0, s0_ref)
    lax.fori_loop(0, qi // 2, pair, 0)

    @pl.when(qi % 2 == 0)
    def _():
        consume(qi, s0_ref, True)

    @pl.when(qi % 2 == 1)
    def _():
        scores(qi, s1_ref)
        consume(qi - 1, s0_ref, False)
        consume(qi, s1_ref, True)

    for hd in range(hg):
        o_ref[0, :, hd * dv:(hd + 1) * dv] = (acc_ref[hd, :, :dv] / acc_ref[hd, :, dv:]).astype(o_ref.dtype)


def _attention(q, kt, v, tq, hg):
    b, heads, s, dq = q.shape
    dve = v.shape[-1]
    dv = dve - LANES
    return pl.pallas_call(
        functools.partial(_attn_kernel, tq=tq),
        grid=(b, heads // hg, s // tq),
        in_specs=[
            pl.BlockSpec((1, hg, tq, dq), lambda bi, hi, qi: (bi, hi, qi, 0)),
            pl.BlockSpec((1, hg, dq, s), lambda bi, hi, qi: (bi, hi, 0, 0)),
            pl.BlockSpec((1, hg, s, dve), lambda bi, hi, qi: (bi, hi, 0, 0)),
        ],
        out_specs=pl.BlockSpec((1, tq, hg * dv), lambda bi, hi, qi: (bi, qi, hi)),
        out_shape=jax.ShapeDtypeStruct((b, s, heads * dv), BF16),
        scratch_shapes=[
            pltpu.VMEM((hg, tq, dve), F32),
            pltpu.VMEM((hg, tq, LANES), F32),
            pltpu.VMEM((hg, tq, dve), F32),
            pltpu.VMEM((hg, tq, tq), F32),
            pltpu.VMEM((hg, tq, tq), F32),
            pltpu.VMEM((hg, tq, tq), BF16),
        ],
        compiler_params=_compiler_params(("arbitrary", "arbitrary", "arbitrary")),
        name="mla_attention",
    )(q, kt, v)


def _ffn_kernel(x_ref, g_ref, w_up_ref, cw_ref, cb_ref, w_down_ref, o_ref, tail_ref, shift_ref,
                *, d_ff, fc):
    @pl.when(pl.program_id(1) == 0)
    def _():
        tail_ref[...] = jnp.zeros_like(tail_ref)

    x = x_ref[0]
    tm, d = x.shape
    h = (x * _rms_scale(x, d) * g_ref[...]).astype(BF16)

    def conv(u, col0):
        cols = slice(col0, col0 + fc)
        shift_ref[SUBLANES:SUBLANES + tm, :] = u
        shift_ref[0:SUBLANES, :] = tail_ref[:, cols]
        u1 = shift_ref[SUBLANES - 1:SUBLANES - 1 + tm, :]
        u2 = shift_ref[SUBLANES - 2:SUBLANES - 2 + tm, :]
        tail_ref[:, cols] = u[tm - SUBLANES:tm, :]
        w = cw_ref[:, cols]
        return cb_ref[:, cols] + u2 * w[0:1] + u1 * w[1:2] + u * w[2:3]

    acc = jnp.zeros((tm, d), F32)
    for j in range(d_ff // fc):
        c0 = j * fc
        u_gate = jnp.dot(h, w_up_ref[:, c0:c0 + fc], preferred_element_type=F32)
        gate = conv(u_gate, c0)
        u_up = jnp.dot(h, w_up_ref[:, d_ff + c0:d_ff + c0 + fc], preferred_element_type=F32)
        up = conv(u_up, d_ff + c0)
        act = (gate * _sigmoid(gate) * up).astype(BF16)
        acc = acc + jnp.dot(act, w_down_ref[c0:c0 + fc, :], preferred_element_type=F32)
    o_ref[0] = x + acc


def _ffn(x, gain, w_up, conv_w, conv_b, w_down, tm, fc):
    b, s, d = x.shape
    d_ff = w_down.shape[0]
    const = lambda bi, si: (0, 0)
    row = pl.BlockSpec((1, tm, d), lambda bi, si: (bi, si, 0))
    return pl.pallas_call(
        functools.partial(_ffn_kernel, d_ff=d_ff, fc=fc),
        grid=(b, s // tm),
        in_specs=[
            row,
            _resident((1, d), const),
            _resident(w_up.shape, const),
            _resident(conv_w.shape, const),
            _resident(conv_b.shape, const),
            _resident(w_down.shape, const),
        ],
        out_specs=row,
        out_shape=jax.ShapeDtypeStruct((b, s, d), F32),
        scratch_shapes=[
            pltpu.VMEM((SUBLANES, 2 * d_ff), F32),
            pltpu.VMEM((SUBLANES + tm, fc), F32),
        ],
        compiler_params=_compiler_params(("arbitrary", "arbitrary")),
        name="conv_ffn",
    )(x, gain, w_up, conv_w, conv_b, w_down)


def _spread_rope(w):
    half = MLA_ROPE // 2
    z = jnp.zeros(w.shape[:-1] + (LANES // 2 - half,), w.dtype)
    return jnp.concatenate([w[..., :half], z, w[..., half:], z], axis=-1)


def _pad_qk_channels(w, heads):
    w = w.reshape(w.shape[:-1] + (heads, MLA_QK))
    w = jnp.concatenate([w[..., :MLA_NOPE], _spread_rope(w[..., MLA_NOPE:])], axis=-1)
    return w.reshape(w.shape[:-2] + (heads * MLA_QK_PAD,))


def kernel(x, positions, norm_mix, norm_ffn, hgrn_w_in, hgrn_lower_bounds, hgrn_out_norm, hgrn_w_out,
           mla_w_in, mla_q_a_norm, mla_w_q_up, mla_kv_a_norm, mla_w_kv_up, mla_q_norm, mla_k_norm,
           mla_w_out, ffn_w_up, ffn_conv_w, ffn_conv_b, ffn_w_down):
    b, s, d = x.shape
    depth = norm_mix.shape[0]
    n_mixers = 2
    q_lora = mla_q_a_norm.shape[1]
    kv_lora = mla_kv_a_norm.shape[1]
    heads = mla_w_q_up.shape[2] // MLA_QK

    lb_soft = jax.nn.softmax(hgrn_lower_bounds.astype(F32), axis=0)
    lower_bounds = jnp.cumsum(lb_soft, axis=0) - lb_soft[0:1]
    inv_freq = ROPE_THETA ** (-jnp.arange(0, MLA_ROPE, 2, dtype=F32) / MLA_ROPE)
    ang = positions.astype(F32)[..., None] * inv_freq
    cos, sin = jnp.cos(ang), jnp.sin(ang)
    zeros = jnp.zeros_like(cos)
    cos_t = jnp.concatenate([cos, zeros, cos, zeros], axis=-1)
    sin_t = jnp.concatenate([-sin, zeros, sin, zeros], axis=-1)

    x = x.astype(F32)
    for layer in range(depth):
        j = layer // n_mixers
        gain = norm_mix[layer].reshape(1, d).astype(F32)
        if layer % n_mixers == 0:
            q, lf, k, v, gate = _hgrn_in(
                x.reshape(b * s, d), gain, hgrn_w_in[j].astype(BF16),
                lower_bounds[j].reshape(1, d), tm=256)
            shp = (b, s, d)
            o = _hgrn_rec(q.reshape(shp), lf.reshape(shp), k.reshape(shp), v.reshape(shp),
                          gate.reshape(shp), hgrn_out_norm[j].reshape(1, HGRN_HEAD).astype(F32), tt=512)
            w_out = hgrn_w_out[j]
        else:
            w_in = jnp.concatenate(
                [mla_w_in[j][:, :q_lora + kv_lora], _spread_rope(mla_w_in[j][:, q_lora + kv_lora:])], axis=-1)
            qh, kh, vh = _mla_in(
                x, cos_t, sin_t, gain, w_in.astype(BF16),
                mla_q_a_norm[j].reshape(1, q_lora).astype(F32),
                _pad_qk_channels(mla_w_q_up[j], heads).astype(BF16),
                mla_kv_a_norm[j].reshape(1, kv_lora).astype(F32),
                mla_w_kv_up[j].astype(BF16),
                _pad_qk_channels(mla_q_norm[j].reshape(1, MLA_QK), 1).astype(F32),
                _pad_qk_channels(mla_k_norm[j].reshape(1, MLA_QK), 1).astype(F32),
                heads=heads, tm=256)
            o = _attention(qh, kh, vh, tq=512, hg=2)
            w_out = mla_w_out[j]
        x = _matmul_residual(x.reshape(b * s, d), o.reshape(b * s, d), w_out.astype(BF16), tm=512)
        x = _ffn(x.reshape(b, s, d), norm_ffn[layer].reshape(1, d).astype(F32),
                 ffn_w_up[layer].astype(BF16), ffn_conv_w[layer].astype(F32),
                 ffn_conv_b[layer].reshape(1, -1).astype(F32), ffn_w_down[layer].astype(BF16),
                 tm=512, fc=256)
    return x
```

```python
import functools
import math

import jax
import jax.numpy as jnp
from jax import lax
from jax.experimental import pallas as pl
from jax.experimental.pallas import tpu as pltpu

F32 = jnp.float32
BF16 = jnp.bfloat16

RMS_EPS = 1e-6
ROPE_THETA = 10000.0

LANES = 128
SUBLANES = 8
VMEM_LIMIT_BYTES = 56 * 1024 * 1024

HGRN_HEAD = 128
HGRN_CHUNK = 64
HGRN_SUB = 16
MLA_NOPE = 128
MLA_ROPE = 64
MLA_QK = MLA_NOPE + MLA_ROPE
MLA_V = 128
MLA_QK_PAD = 2 * LANES
FFN_CONV = 3
ATTN_ROW_GROUP = 64
NEG_BIG = -1e30
MLA_Q_SCALE = (MLA_QK ** -0.5) * math.log2(math.e)
MLA_SCORE_MAX = 60.0

_NT = (((1,), (1,)), ((), ()))


def _compiler_params(semantics):
    return pltpu.CompilerParams(dimension_semantics=semantics, vmem_limit_bytes=VMEM_LIMIT_BYTES)


def _resident(shape, index_map):
    return pl.BlockSpec(shape, index_map, pipeline_mode=pl.Buffered(1))


def _rms_scale(x, n):
    return lax.rsqrt(jnp.sum(x * x, axis=-1, keepdims=True) * (1.0 / n) + RMS_EPS)


def _sigmoid(x):
    return 1.0 / (1.0 + jnp.exp(-x))


def _hgrn_in_kernel(x_ref, g_ref, w_ref, lb_ref, q_ref, lf_ref, k_ref, v_ref, gate_ref):
    x = x_ref[...]
    d = x.shape[-1]
    h = (x * _rms_scale(x, d) * g_ref[...]).astype(BF16)
    p = jnp.dot(h, w_ref[...], preferred_element_type=F32)
    q = p[:, 0 * d:1 * d]
    f = p[:, 1 * d:2 * d]
    lb = lb_ref[...]
    forget = lb + (1.0 - lb) * _sigmoid(f)
    q_ref[...] = q * _sigmoid(q)
    lf_ref[...] = jnp.log(forget)
    k_ref[...] = 1.0 - forget
    v_ref[...] = p[:, 2 * d:3 * d].astype(BF16)
    g = p[:, 3 * d:4 * d]
    gate_ref[...] = g * _sigmoid(g)


def _hgrn_in(x2d, gain, w_in, lb, tm):
    m, d = x2d.shape
    row = pl.BlockSpec((tm, d), lambda i: (i, 0))
    vec = _resident((1, d), lambda i: (0, 0))
    return pl.pallas_call(
        _hgrn_in_kernel,
        grid=(m // tm,),
        in_specs=[row, vec, _resident((d, 4 * d), lambda i: (0, 0)), vec],
        out_specs=[row, row, row, row, row],
        out_shape=[
            jax.ShapeDtypeStruct((m, d), F32),
            jax.ShapeDtypeStruct((m, d), F32),
            jax.ShapeDtypeStruct((m, d), F32),
            jax.ShapeDtypeStruct((m, d), BF16),
            jax.ShapeDtypeStruct((m, d), F32),
        ],
        compiler_params=_compiler_params(("arbitrary",)),
        name="hgrn_in",
    )(x2d, gain, w_in, lb)


def _cumsum_rows(x):
    c = x.shape[0]
    row = lax.broadcasted_iota(jnp.int32, (c, c), 0)
    col = lax.broadcasted_iota(jnp.int32, (c, c), 1)
    tri = jnp.where(col <= row, 1.0, 0.0).astype(BF16)
    hi = x.astype(BF16)
    rem = x - hi.astype(F32)
    mid = rem.astype(BF16)
    lo = (rem - mid.astype(F32)).astype(BF16)
    out = jnp.dot(tri, hi, preferred_element_type=F32)
    out = out + jnp.dot(tri, mid, preferred_element_type=F32)
    return out + jnp.dot(tri, lo, preferred_element_type=F32)


def _hgrn_chunk(q, lf, k, v, gate, gain, state_t):
    c = HGRN_CHUNK
    sub = HGRN_SUB
    half = SUBLANES
    g_cum = _cumsum_rows(lf)
    g_last = g_cum[c - 1:c, :]

    q_dec = (q * jnp.exp(g_cum)).astype(BF16)
    o = lax.dot_general(q_dec, state_t.astype(BF16), _NT, preferred_element_type=F32)

    k_dec = (k * jnp.exp(g_last - g_cum)).astype(BF16)
    v_t = v.astype(F32).T.astype(BF16)
    new_state_t = state_t * jnp.exp(g_last) + jnp.dot(v_t, k_dec, preferred_element_type=F32)

    row_c = lax.broadcasted_iota(jnp.int32, (c, HGRN_HEAD), 0)
    lane = lax.broadcasted_iota(jnp.int32, (half, c), 1)
    sub_row = lax.broadcasted_iota(jnp.int32, (half, HGRN_HEAD), 0)
    blocks = []
    for j in range(c // sub):
        r0 = j * sub
        qb = q[r0:r0 + sub]
        kb = k[r0:r0 + sub]
        gb = g_cum[r0:r0 + sub]
        if j == 0:
            a_top = jnp.zeros((half, c), F32)
            a_bot = jnp.zeros((half, c), F32)
        else:
            g_ref = g_cum[r0 - 1:r0, :]
            q_off = (qb * jnp.exp(gb - g_ref)).astype(BF16)
            k_off = k * jnp.exp(jnp.minimum(g_ref - g_cum, 0.0))
            k_off = jnp.where(row_c < r0, k_off, 0.0).astype(BF16)
            a_off = lax.dot_general(q_off, k_off, _NT, preferred_element_type=F32)
            a_top = a_off[:half]
            a_bot = a_off[half:]
        q_top, q_bot = qb[:half], qb[half:]
        g_top, g_bot = gb[:half], gb[half:]
        for s in range(sub):
            gs = gb[s:s + 1]
            ks = kb[s:s + 1]
            if s < half:
                e_top = jnp.exp(jnp.where(sub_row >= s, g_top - gs, NEG_BIG))
                col_top = jnp.sum(q_top * (ks * e_top), axis=-1, keepdims=True)
                a_top = jnp.where(lane == r0 + s, col_top, a_top)
                e_bot = jnp.exp(g_bot - gs)
            else:
                e_bot = jnp.exp(jnp.where(sub_row >= s - half, g_bot - gs, NEG_BIG))
            col_bot = jnp.sum(q_bot * (ks * e_bot), axis=-1, keepdims=True)
            a_bot = jnp.where(lane == r0 + s, col_bot, a_bot)
        blocks += [a_top, a_bot]
    a = jnp.concatenate(blocks, axis=0).astype(BF16)
    o = o + jnp.dot(a, v, preferred_element_type=F32)

    y = o * _rms_scale(o, HGRN_HEAD) * gain
    return (y * gate).astype(BF16), new_state_t


def _hgrn_rec_kernel(q_ref, lf_ref, k_ref, v_ref, gate_ref, gain_ref, o_ref, state_ref):
    @pl.when(pl.program_id(2) == 0)
    def _():
        state_ref[...] = jnp.zeros_like(state_ref)

    gain = gain_ref[...]
    state_t = state_ref[...]
    for ci in range(q_ref.shape[1] // HGRN_CHUNK):
        rows = slice(ci * HGRN_CHUNK, (ci + 1) * HGRN_CHUNK)
        o_ref[0, rows, :], state_t = _hgrn_chunk(
            q_ref[0, rows, :], lf_ref[0, rows, :], k_ref[0, rows, :], v_ref[0, rows, :],
            gate_ref[0, rows, :], gain, state_t)
    state_ref[...] = state_t


def _hgrn_rec(q, lf, k, v, gate, gain, tt):
    b, s, d = q.shape
    heads = d // HGRN_HEAD
    blk = pl.BlockSpec((1, tt, HGRN_HEAD), lambda bi, hi, ti: (bi, ti, hi))
    return pl.pallas_call(
        _hgrn_rec_kernel,
        grid=(b, heads, s // tt),
        in_specs=[blk, blk, blk, blk, blk, pl.BlockSpec((1, HGRN_HEAD), lambda bi, hi, ti: (0, 0))],
        out_specs=blk,
        out_shape=jax.ShapeDtypeStruct((b, s, d), BF16),
        scratch_shapes=[pltpu.VMEM((HGRN_HEAD, HGRN_HEAD), F32)],
        compiler_params=_compiler_params(("arbitrary", "arbitrary", "arbitrary")),
        name="hgrn_rec",
    )(q, lf, k, v, gate, gain)


def _matmul_residual_kernel(x_ref, a_ref, w_ref, o_ref):
    o_ref[...] = x_ref[...] + jnp.dot(a_ref[...], w_ref[...], preferred_element_type=F32)


def _matmul_residual(x2d, a2d, w, tm):
    m, d = x2d.shape
    kdim = a2d.shape[1]
    return pl.pallas_call(
        _matmul_residual_kernel,
        grid=(m // tm,),
        in_specs=[
            pl.BlockSpec((tm, d), lambda i: (i, 0)),
            pl.BlockSpec((tm, kdim), lambda i: (i, 0)),
            _resident((kdim, d), lambda i: (0, 0)),
        ],
        out_specs=pl.BlockSpec((tm, d), lambda i: (i, 0)),
        out_shape=jax.ShapeDtypeStruct((m, d), F32),
        compiler_params=_compiler_params(("arbitrary",)),
        name="matmul_residual",
    )(x2d, a2d, w)


def _rope(t, cos_t, sin_t):
    return t * cos_t + pltpu.roll(t, LANES // 2, axis=1) * sin_t


def _mla_in_kernel(x_ref, cos_ref, sin_ref, g_ref, w_in_ref, qa_ref, w_q_ref, kva_ref, w_kv_ref,
                   qn_ref, kn_ref, q_out, kt_out, v_out, *, q_scale, q_lora, kv_lora):
    x = x_ref[0]
    d = x.shape[-1]
    h = (x * _rms_scale(x, d) * g_ref[...]).astype(BF16)
    c = jnp.dot(h, w_in_ref[...], preferred_element_type=F32)
    c_q = c[:, :q_lora]
    c_kv = c[:, q_lora:q_lora + kv_lora]
    k_rope = c[:, q_lora + kv_lora:]
    c_q = (c_q * _rms_scale(c_q, q_lora) * qa_ref[...]).astype(BF16)
    c_kv = (c_kv * _rms_scale(c_kv, kv_lora) * kva_ref[...]).astype(BF16)
    q = jnp.dot(c_q, w_q_ref[...], preferred_element_type=F32)
    kv = jnp.dot(c_kv, w_kv_ref[...], preferred_element_type=F32)

    cos_t = cos_ref[0]
    sin_t = sin_ref[0]
    qn = qn_ref[...]
    kn = kn_ref[...]
    qn_nope, qn_rope = qn[:, :LANES], qn[:, LANES:]
    kn_nope, kn_rope = kn[:, :LANES], kn[:, LANES:]
    k_rope_ss = jnp.sum(k_rope * k_rope, axis=-1, keepdims=True)
    heads = q_out.shape[1]
    for hd in range(heads):
        base = hd * MLA_QK_PAD
        q_nope = q[:, base:base + LANES]
        q_rope = q[:, base + LANES:base + 2 * LANES]
        ss = jnp.sum(q_nope * q_nope, axis=-1, keepdims=True) + jnp.sum(q_rope * q_rope, axis=-1, keepdims=True)
        r = lax.rsqrt(ss * (1.0 / MLA_QK) + RMS_EPS)
        q_out[0, hd, :, :LANES] = (q_nope * r * qn_nope * q_scale).astype(BF16)
        q_out[0, hd, :, LANES:] = (_rope(q_rope * r * qn_rope, cos_t, sin_t) * q_scale).astype(BF16)

        k_nope = kv[:, base:base + LANES]
        ssk = jnp.sum(k_nope * k_nope, axis=-1, keepdims=True) + k_rope_ss
        rk = lax.rsqrt(ssk * (1.0 / MLA_QK) + RMS_EPS)
        kt_out[0, hd, :LANES, :] = (k_nope * rk * kn_nope).T.astype(BF16)
        kt_out[0, hd, LANES:, :] = _rope(k_rope * rk * kn_rope, cos_t, sin_t).T.astype(BF16)
        v_out[0, hd, :, :MLA_V] = kv[:, base + LANES:base + 2 * LANES].astype(BF16)
        v_out[0, hd, :, MLA_V:] = jnp.ones((x.shape[0], LANES), BF16)


def _mla_in(x, cos_t, sin_t, gain, w_in, qa, w_q, kva, w_kv, qn, kn, heads, tm):
    b, s, d = x.shape
    q_lora = qa.shape[1]
    kv_lora = kva.shape[1]
    q_scale = MLA_Q_SCALE
    const = lambda bi, si: (0, 0)
    row = lambda w: pl.BlockSpec((1, tm, w), lambda bi, si: (bi, si, 0))
    head_out = lambda w: pl.BlockSpec((1, heads, tm, w), lambda bi, si: (bi, 0, si, 0))
    return pl.pallas_call(
        functools.partial(_mla_in_kernel, q_scale=q_scale, q_lora=q_lora, kv_lora=kv_lora),
        grid=(b, s // tm),
        in_specs=[
            row(d), row(LANES), row(LANES),
            _resident((1, d), const),
            _resident(w_in.shape, const),
            _resident(qa.shape, const),
            _resident(w_q.shape, const),
            _resident(kva.shape, const),
            _resident(w_kv.shape, const),
            _resident(qn.shape, const),
            _resident(kn.shape, const),
        ],
        out_specs=[
            head_out(MLA_QK_PAD),
            pl.BlockSpec((1, heads, MLA_QK_PAD, tm), lambda bi, si: (bi, 0, 0, si)),
            head_out(MLA_V + LANES),
        ],
        out_shape=[
            jax.ShapeDtypeStruct((b, heads, s, MLA_QK_PAD), BF16),
            jax.ShapeDtypeStruct((b, heads, MLA_QK_PAD, s), BF16),
            jax.ShapeDtypeStruct((b, heads, s, MLA_V + LANES), BF16),
        ],
        compiler_params=_compiler_params(("arbitrary", "arbitrary")),
        name="mla_in",
    )(x, cos_t, sin_t, gain, w_in, qa, w_q, kva, w_kv, qn, kn)


def _attn_online_kernel(q_ref, kt_ref, v_ref, o_ref, acc_ref, m_ref, alpha_ref, s0_ref, s1_ref, p_ref,
                        *, tq):
    qi = pl.program_id(2)
    hg = q_ref.shape[1]
    dv = v_ref.shape[-1] - LANES
    lane_tiles = tq // LANES

    def scores(ki, s_ref):
        cols = pl.ds(pl.multiple_of(ki * tq, tq), tq)
        for hd in range(hg):
            s_ref[hd] = jnp.dot(q_ref[0, hd], kt_ref[0, hd, :, cols], preferred_element_type=F32)

    def consume(ki, s_ref, masked):
        rows = pl.ds(pl.multiple_of(ki * tq, tq), tq)
        for hd in range(hg):
            for g in range(tq // ATTN_ROW_GROUP):
                rg = slice(g * ATTN_ROW_GROUP, (g + 1) * ATTN_ROW_GROUP)
                s = s_ref[hd, rg, :]
                if masked:
                    r = lax.broadcasted_iota(jnp.int32, s.shape, 0) + g * ATTN_ROW_GROUP
                    c = lax.broadcasted_iota(jnp.int32, s.shape, 1)
                    s = jnp.where(c <= r, s, NEG_BIG)
                m_prev = m_ref[hd, rg, :]
                m_new = jnp.maximum(m_prev, jnp.max(s, axis=-1, keepdims=True))
                alpha = jnp.exp2(m_prev - m_new)
                m_ref[hd, rg, :] = m_new
                alpha_ref[hd, rg, :LANES] = alpha
                alpha_ref[hd, rg, LANES:] = alpha
                m_wide = jnp.concatenate([m_new] * lane_tiles, axis=1)
                p_ref[hd, rg, :] = jnp.exp2(s - m_wide).astype(BF16)
            pv = jnp.dot(p_ref[hd], v_ref[0, hd, rows, :], preferred_element_type=F32)
            acc_ref[hd] = alpha_ref[hd] * acc_ref[hd] + pv

    def pair(kp, carry):
        k0 = 2 * kp
        scores(k0 + 1, s1_ref)
        consume(k0, s0_ref, False)
        scores(k0 + 2, s0_ref)
        consume(k0 + 1, s1_ref, False)
        return carry

    acc_ref[...] = jnp.zeros_like(acc_ref)
    m_ref[...] = jnp.full(m_ref.shape, NEG_BIG, F32)
    scores(0, s0_ref)
    lax.fori_loop(0, qi // 2, pair, 0)

    @pl.when(qi % 2 == 0)
    def _():
        consume(qi, s0_ref, True)

    @pl.when(qi % 2 == 1)
    def _():
        scores(qi, s1_ref)
        consume(qi - 1, s0_ref, False)
        consume(qi, s1_ref, True)

    for hd in range(hg):
        o_ref[0, :, hd * dv:(hd + 1) * dv] = (acc_ref[hd, :, :dv] / acc_ref[hd, :, dv:]).astype(o_ref.dtype)


def _attn_bounded_kernel(q_ref, kt_ref, v_ref, o_ref, acc_ref, p0_ref, p1_ref, pd_ref, *, tq):
    qi = pl.program_id(2)
    hg = q_ref.shape[1]
    dv = v_ref.shape[-1] - LANES

    def probs(ki, p_ref, masked):
        cols = pl.ds(pl.multiple_of(ki * tq, tq), tq)
        for hd in range(hg):
            s = jnp.dot(q_ref[0, hd], kt_ref[0, hd, :, cols], preferred_element_type=F32)
            if masked:
                r = lax.broadcasted_iota(jnp.int32, s.shape, 0)
                c = lax.broadcasted_iota(jnp.int32, s.shape, 1)
                s = jnp.where(c <= r, s, NEG_BIG)
            p_ref[hd] = jnp.exp2(s).astype(BF16)

    def accumulate(ki, p_ref):
        rows = pl.ds(pl.multiple_of(ki * tq, tq), tq)
        for hd in range(hg):
            acc_ref[hd] += jnp.dot(p_ref[hd], v_ref[0, hd, rows, :], preferred_element_type=F32)

    def pair(kp, carry):
        k0 = 2 * kp
        probs(k0 + 1, p1_ref, False)
        accumulate(k0, p0_ref)
        probs(k0 + 2, p0_ref, False)
        accumulate(k0 + 1, p1_ref)
        return carry

    acc_ref[...] = jnp.zeros_like(acc_ref)

    @pl.when(qi == 0)
    def _():
        probs(0, pd_ref, True)
        accumulate(0, pd_ref)

    @pl.when(qi > 0)
    def _():
        probs(0, p0_ref, False)
        lax.fori_loop(0, (qi - 1) // 2, pair, 0)

        @pl.when(qi % 2 == 1)
        def _():
            probs(qi, pd_ref, True)
            accumulate(qi - 1, p0_ref)
            accumulate(qi, pd_ref)

        @pl.when(qi % 2 == 0)
        def _():
            probs(qi - 1, p1_ref, False)
            accumulate(qi - 2, p0_ref)
            probs(qi, pd_ref, True)
            accumulate(qi - 1, p1_ref)
            accumulate(qi, pd_ref)

    for hd in range(hg):
        o_ref[0, :, hd * dv:(hd + 1) * dv] = (acc_ref[hd, :, :dv] / acc_ref[hd, :, dv:]).astype(o_ref.dtype)


def _attention(q, kt, v, tq, hg, bounded):
    b, heads, s, dq = q.shape
    dve = v.shape[-1]
    dv = dve - LANES
    acc = pltpu.VMEM((hg, tq, dve), F32)
    probs = pltpu.VMEM((hg, tq, tq), BF16)
    scores = pltpu.VMEM((hg, tq, tq), F32)
    if bounded:
        body, scratch = _attn_bounded_kernel, [acc, probs, probs, probs]
    else:
        running_max = pltpu.VMEM((hg, tq, LANES), F32)
        rescale = pltpu.VMEM((hg, tq, dve), F32)
        body, scratch = _attn_online_kernel, [acc, running_max, rescale, scores, scores, probs]
    return pl.pallas_call(
        functools.partial(body, tq=tq),
        grid=(b, heads // hg, s // tq),
        in_specs=[
            pl.BlockSpec((1, hg, tq, dq), lambda bi, hi, qi: (bi, hi, qi, 0)),
            pl.BlockSpec((1, hg, dq, s), lambda bi, hi, qi: (bi, hi, 0, 0)),
            pl.BlockSpec((1, hg, s, dve), lambda bi, hi, qi: (bi, hi, 0, 0)),
        ],
        out_specs=pl.BlockSpec((1, tq, hg * dv), lambda bi, hi, qi: (bi, qi, hi)),
        out_shape=jax.ShapeDtypeStruct((b, s, heads * dv), BF16),
        scratch_shapes=scratch,
        compiler_params=_compiler_params(("arbitrary", "arbitrary", "arbitrary")),
        name="mla_attention_bounded" if bounded else "mla_attention_online",
    )(q, kt, v)


def _ffn_kernel(x_ref, g_ref, w_up_ref, cw_ref, cb_ref, w_down_ref, o_ref, tail_ref, shift_ref,
                *, d_ff, fc):
    @pl.when(pl.program_id(1) == 0)
    def _():
        tail_ref[...] = jnp.zeros_like(tail_ref)

    x = x_ref[0]
    tm, d = x.shape
    h = (x * _rms_scale(x, d) * g_ref[...]).astype(BF16)

    def conv(u, col0):
        cols = slice(col0, col0 + fc)
        stage = shift_ref.at[col0 // fc]
        stage[SUBLANES:SUBLANES + tm, :] = u
        stage[0:SUBLANES, :] = tail_ref[:, cols]
        u1 = stage[SUBLANES - 1:SUBLANES - 1 + tm, :]
        u2 = stage[SUBLANES - 2:SUBLANES - 2 + tm, :]
        tail_ref[:, cols] = u[tm - SUBLANES:tm, :]
        w = cw_ref[:, cols]
        return cb_ref[:, cols] + u2 * w[0:1] + u1 * w[1:2] + u * w[2:3]

    acc = jnp.zeros((tm, d), F32)
    for j in range(d_ff // fc):
        c0 = j * fc
        u_gate = jnp.dot(h, w_up_ref[:, c0:c0 + fc], preferred_element_type=F32)
        gate = conv(u_gate, c0)
        u_up = jnp.dot(h, w_up_ref[:, d_ff + c0:d_ff + c0 + fc], preferred_element_type=F32)
        up = conv(u_up, d_ff + c0)
        act = (gate * _sigmoid(gate) * up).astype(BF16)
        acc = acc + jnp.dot(act, w_down_ref[c0:c0 + fc, :], preferred_element_type=F32)
    o_ref[0] = x + acc


def _ffn(x, gain, w_up, conv_w, conv_b, w_down, tm, fc):
    b, s, d = x.shape
    d_ff = w_down.shape[0]
    const = lambda bi, si: (0, 0)
    row = pl.BlockSpec((1, tm, d), lambda bi, si: (bi, si, 0))
    return pl.pallas_call(
        functools.partial(_ffn_kernel, d_ff=d_ff, fc=fc),
        grid=(b, s // tm),
        in_specs=[
            row,
            _resident((1, d), const),
            _resident(w_up.shape, const),
            _resident(conv_w.shape, const),
            _resident(conv_b.shape, const),
            _resident(w_down.shape, const),
        ],
        out_specs=row,
        out_shape=jax.ShapeDtypeStruct((b, s, d), F32),
        scratch_shapes=[
            pltpu.VMEM((SUBLANES, 2 * d_ff), F32),
            pltpu.VMEM((2 * d_ff // fc, SUBLANES + tm, fc), F32),
        ],
        compiler_params=_compiler_params(("arbitrary", "arbitrary")),
        name="conv_ffn",
    )(x, gain, w_up, conv_w, conv_b, w_down)


def _spread_rope(w):
    half = MLA_ROPE // 2
    z = jnp.zeros(w.shape[:-1] + (LANES // 2 - half,), w.dtype)
    return jnp.concatenate([w[..., :half], z, w[..., half:], z], axis=-1)


def _pad_qk_channels(w, heads):
    w = w.reshape(w.shape[:-1] + (heads, MLA_QK))
    w = jnp.concatenate([w[..., :MLA_NOPE], _spread_rope(w[..., MLA_NOPE:])], axis=-1)
    return w.reshape(w.shape[:-2] + (heads * MLA_QK_PAD,))


def kernel(x, positions, norm_mix, norm_ffn, hgrn_w_in, hgrn_lower_bounds, hgrn_out_norm, hgrn_w_out,
           mla_w_in, mla_q_a_norm, mla_w_q_up, mla_kv_a_norm, mla_w_kv_up, mla_q_norm, mla_k_norm,
           mla_w_out, ffn_w_up, ffn_conv_w, ffn_conv_b, ffn_w_down):
    b, s, d = x.shape
    depth = norm_mix.shape[0]
    n_mixers = 2
    q_lora = mla_q_a_norm.shape[1]
    kv_lora = mla_kv_a_norm.shape[1]
    heads = mla_w_q_up.shape[2] // MLA_QK

    lb_soft = jax.nn.softmax(hgrn_lower_bounds.astype(F32), axis=0)
    lower_bounds = jnp.cumsum(lb_soft, axis=0) - lb_soft[0:1]
    inv_freq = ROPE_THETA ** (-jnp.arange(0, MLA_ROPE, 2, dtype=F32) / MLA_ROPE)
    ang = positions.astype(F32)[..., None] * inv_freq
    cos, sin = jnp.cos(ang), jnp.sin(ang)
    zeros = jnp.zeros_like(cos)
    cos_t = jnp.concatenate([cos, zeros, cos, zeros], axis=-1)
    sin_t = jnp.concatenate([-sin, zeros, sin, zeros], axis=-1)

    x = x.astype(F32)
    for layer in range(depth):
        j = layer // n_mixers
        gain = norm_mix[layer].reshape(1, d).astype(F32)
        if layer % n_mixers == 0:
            q, lf, k, v, gate = _hgrn_in(
                x.reshape(b * s, d), gain, hgrn_w_in[j].astype(BF16),
                lower_bounds[j].reshape(1, d), tm=256)
            shp = (b, s, d)
            o = _hgrn_rec(q.reshape(shp), lf.reshape(shp), k.reshape(shp), v.reshape(shp),
                          gate.reshape(shp), hgrn_out_norm[j].reshape(1, HGRN_HEAD).astype(F32), tt=512)
            w_out = hgrn_w_out[j]
        else:
            w_in = jnp.concatenate(
                [mla_w_in[j][:, :q_lora + kv_lora], _spread_rope(mla_w_in[j][:, q_lora + kv_lora:])], axis=-1)
            bound = 1.02 * MLA_QK * MLA_Q_SCALE * jnp.max(jnp.abs(mla_q_norm[j])) * jnp.max(jnp.abs(mla_k_norm[j]))
            qh, kth, vh = _mla_in(
                x, cos_t, sin_t, gain, w_in.astype(BF16),
                mla_q_a_norm[j].reshape(1, q_lora).astype(F32),
                _pad_qk_channels(mla_w_q_up[j], heads).astype(BF16),
                mla_kv_a_norm[j].reshape(1, kv_lora).astype(F32),
                mla_w_kv_up[j].astype(BF16),
                _pad_qk_channels(mla_q_norm[j].reshape(1, MLA_QK), 1).astype(F32),
                _pad_qk_channels(mla_k_norm[j].reshape(1, MLA_QK), 1).astype(F32),
                heads=heads, tm=256)
            o = lax.cond(
                bound <= MLA_SCORE_MAX,
                functools.partial(_attention, tq=512, hg=2, bounded=True),
                functools.partial(_attention, tq=512, hg=2, bounded=False),
                qh, kth, vh)
            w_out = mla_w_out[j]
        x = _matmul_residual(x.reshape(b * s, d), o.reshape(b * s, d), w_out.astype(BF16), tm=512)
        x = _ffn(x.reshape(b, s, d), norm_ffn[layer].reshape(1, d).astype(F32),
                 ffn_w_up[layer].astype(BF16), ffn_conv_w[layer].astype(F32),
                 ffn_conv_b[layer].reshape(1, -1).astype(F32), ffn_w_down[layer].astype(BF16),
                 tm=512, fc=256)
    return x
```

```python
import functools
import math

import jax
import jax.numpy as jnp
from jax import lax
from jax.experimental import pallas as pl
from jax.experimental.pallas import tpu as pltpu

F32 = jnp.float32
BF16 = jnp.bfloat16

RMS_EPS = 1e-6
ROPE_THETA = 10000.0

LANES = 128
SUBLANES = 8
VMEM_LIMIT_BYTES = 56 * 1024 * 1024

HGRN_HEAD = 128
HGRN_CHUNK = 64
HGRN_SUB = 16
MLA_NOPE = 128
MLA_ROPE = 64
MLA_QK = MLA_NOPE + MLA_ROPE
MLA_V = 128
MLA_QK_PAD = 2 * LANES
FFN_CONV = 3
ATTN_ROW_GROUP = 64
NEG_BIG = -1e30
MLA_Q_SCALE = (MLA_QK ** -0.5) * math.log2(math.e)
MLA_SCORE_MAX = 60.0

_NT = (((1,), (1,)), ((), ()))


def _compiler_params(semantics):
    return pltpu.CompilerParams(dimension_semantics=semantics, vmem_limit_bytes=VMEM_LIMIT_BYTES)


def _resident(shape, index_map):
    return pl.BlockSpec(shape, index_map, pipeline_mode=pl.Buffered(1))


def _rms_scale(x, n):
    return lax.rsqrt(jnp.sum(x * x, axis=-1, keepdims=True) * (1.0 / n) + RMS_EPS)


def _sigmoid(x):
    return 1.0 / (1.0 + jnp.exp(-x))


def _hgrn_in_kernel(x_ref, g_ref, w_ref, lb_ref, q_ref, lf_ref, k_ref, v_ref, gate_ref):
    x = x_ref[...]
    d = x.shape[-1]
    h = (x * _rms_scale(x, d) * g_ref[...]).astype(BF16)
    p = jnp.dot(h, w_ref[...], preferred_element_type=F32)
    q = p[:, 0 * d:1 * d]
    f = p[:, 1 * d:2 * d]
    lb = lb_ref[...]
    forget = lb + (1.0 - lb) * _sigmoid(f)
    q_ref[...] = q * _sigmoid(q)
    lf_ref[...] = jnp.log2(forget)
    k_ref[...] = 1.0 - forget
    v_ref[...] = p[:, 2 * d:3 * d].astype(BF16)
    g = p[:, 3 * d:4 * d]
    gate_ref[...] = g * _sigmoid(g)


def _hgrn_in(x2d, gain, w_in, lb, tm):
    m, d = x2d.shape
    row = pl.BlockSpec((tm, d), lambda i: (i, 0))
    vec = _resident((1, d), lambda i: (0, 0))
    return pl.pallas_call(
        _hgrn_in_kernel,
        grid=(m // tm,),
        in_specs=[row, vec, _resident((d, 4 * d), lambda i: (0, 0)), vec],
        out_specs=[row, row, row, row, row],
        out_shape=[
            jax.ShapeDtypeStruct((m, d), F32),
            jax.ShapeDtypeStruct((m, d), F32),
            jax.ShapeDtypeStruct((m, d), F32),
            jax.ShapeDtypeStruct((m, d), BF16),
            jax.ShapeDtypeStruct((m, d), F32),
        ],
        compiler_params=_compiler_params(("arbitrary",)),
        name="hgrn_in",
    )(x2d, gain, w_in, lb)


def _cumsum_rows(x):
    c = x.shape[0]
    row = lax.broadcasted_iota(jnp.int32, (c, c), 0)
    col = lax.broadcasted_iota(jnp.int32, (c, c), 1)
    tri = jnp.where(col <= row, 1.0, 0.0).astype(BF16)
    hi = x.astype(BF16)
    rem = x - hi.astype(F32)
    mid = rem.astype(BF16)
    lo = (rem - mid.astype(F32)).astype(BF16)
    out = jnp.dot(tri, hi, preferred_element_type=F32)
    out = out + jnp.dot(tri, mid, preferred_element_type=F32)
    return out + jnp.dot(tri, lo, preferred_element_type=F32)


def _hgrn_chunk(q, lf, k, v, gate, gain, state_t):
    c = HGRN_CHUNK
    sub = HGRN_SUB
    half = SUBLANES
    g_cum = _cumsum_rows(lf)
    g_last = g_cum[c - 1:c, :]

    q_dec = (q * jnp.exp2(g_cum)).astype(BF16)
    o = lax.dot_general(q_dec, state_t.astype(BF16), _NT, preferred_element_type=F32)

    k_dec = (k * jnp.exp2(g_last - g_cum)).astype(BF16)
    v_t = v.astype(F32).T.astype(BF16)
    new_state_t = state_t * jnp.exp2(g_last) + jnp.dot(v_t, k_dec, preferred_element_type=F32)

    row_c = lax.broadcasted_iota(jnp.int32, (c, HGRN_HEAD), 0)
    lane = lax.broadcasted_iota(jnp.int32, (half, c), 1)
    sub_row = lax.broadcasted_iota(jnp.int32, (half, HGRN_HEAD), 0)
    blocks = []
    for j in range(c // sub):
        r0 = j * sub
        qb = q[r0:r0 + sub]
        kb = k[r0:r0 + sub]
        gb = g_cum[r0:r0 + sub]
        if j == 0:
            a_top = jnp.zeros((half, c), F32)
            a_bot = jnp.zeros((half, c), F32)
        else:
            g_ref = g_cum[r0 - 1:r0, :]
            q_off = (qb * jnp.exp2(gb - g_ref)).astype(BF16)
            k_off = k * jnp.exp2(jnp.minimum(g_ref - g_cum, 0.0))
            k_off = jnp.where(row_c < r0, k_off, 0.0).astype(BF16)
            a_off = lax.dot_general(q_off, k_off, _NT, preferred_element_type=F32)
            a_top = a_off[:half]
            a_bot = a_off[half:]
        q_top, q_bot = qb[:half], qb[half:]
        g_top, g_bot = gb[:half], gb[half:]
        for s in range(sub):
            gs = gb[s:s + 1]
            ks = kb[s:s + 1]
            if s < half:
                e_top = jnp.exp2(jnp.where(sub_row >= s, g_top - gs, NEG_BIG))
                col_top = jnp.sum(q_top * (ks * e_top), axis=-1, keepdims=True)
                a_top = jnp.where(lane == r0 + s, col_top, a_top)
                e_bot = jnp.exp2(g_bot - gs)
            else:
                e_bot = jnp.exp2(jnp.where(sub_row >= s - half, g_bot - gs, NEG_BIG))
            col_bot = jnp.sum(q_bot * (ks * e_bot), axis=-1, keepdims=True)
            a_bot = jnp.where(lane == r0 + s, col_bot, a_bot)
        blocks += [a_top, a_bot]
    a = jnp.concatenate(blocks, axis=0).astype(BF16)
    o = o + jnp.dot(a, v, preferred_element_type=F32)

    y = o * _rms_scale(o, HGRN_HEAD) * gain
    return (y * gate).astype(BF16), new_state_t


def _hgrn_rec_kernel(q_ref, lf_ref, k_ref, v_ref, gate_ref, gain_ref, o_ref, state_ref):
    @pl.when(pl.program_id(2) == 0)
    def _():
        state_ref[...] = jnp.zeros_like(state_ref)

    gain = gain_ref[...]
    state_t = state_ref[...]
    for ci in range(q_ref.shape[1] // HGRN_CHUNK):
        rows = slice(ci * HGRN_CHUNK, (ci + 1) * HGRN_CHUNK)
        o_ref[0, rows, :], state_t = _hgrn_chunk(
            q_ref[0, rows, :], lf_ref[0, rows, :], k_ref[0, rows, :], v_ref[0, rows, :],
            gate_ref[0, rows, :], gain, state_t)
    state_ref[...] = state_t


def _hgrn_rec(q, lf, k, v, gate, gain, tt):
    b, s, d = q.shape
    heads = d // HGRN_HEAD
    blk = pl.BlockSpec((1, tt, HGRN_HEAD), lambda bi, hi, ti: (bi, ti, hi))
    return pl.pallas_call(
        _hgrn_rec_kernel,
        grid=(b, heads, s // tt),
        in_specs=[blk, blk, blk, blk, blk, pl.BlockSpec((1, HGRN_HEAD), lambda bi, hi, ti: (0, 0))],
        out_specs=blk,
        out_shape=jax.ShapeDtypeStruct((b, s, d), BF16),
        scratch_shapes=[pltpu.VMEM((HGRN_HEAD, HGRN_HEAD), F32)],
        compiler_params=_compiler_params(("arbitrary", "arbitrary", "arbitrary")),
        name="hgrn_rec",
    )(q, lf, k, v, gate, gain)


def _rope(t, cos_t, sin_t):
    return t * cos_t + pltpu.roll(t, LANES // 2, axis=1) * sin_t


def _mla_in_kernel(x_ref, cos_ref, sin_ref, cost_ref, sint_ref, g_ref, w_in_ref, qa_ref, w_q_ref, kva_ref,
                   w_knt_ref, w_v_ref, qn_ref, kn_nope_ref, kn_rope_ref, q_out, kt_out, v_out,
                   *, q_lora, kv_lora):
    x = x_ref[0]
    tm, d = x.shape
    heads = q_out.shape[1]
    quarter = LANES // 4
    h = (x * _rms_scale(x, d) * g_ref[...]).astype(BF16)
    c = jnp.dot(h, w_in_ref[...], preferred_element_type=F32)
    c_q = c[:, :q_lora]
    c_kv = c[:, q_lora:q_lora + kv_lora]
    k_rope = c[:, q_lora + kv_lora:]
    c_q = (c_q * _rms_scale(c_q, q_lora) * qa_ref[...]).astype(BF16)
    c_kv = (c_kv * _rms_scale(c_kv, kv_lora) * kva_ref[...]).astype(BF16)
    q = jnp.dot(c_q, w_q_ref[...], preferred_element_type=F32)
    v_all = jnp.dot(c_kv, w_v_ref[...], preferred_element_type=F32)
    kn_t = lax.dot_general(w_knt_ref[...], c_kv, _NT, preferred_element_type=F32)

    q_tiles = [q[:, hd * MLA_QK_PAD:(hd + 1) * MLA_QK_PAD] for hd in range(heads)]
    q_sq = jnp.concatenate([(t * t).astype(BF16) for t in q_tiles], axis=0)
    ss_q = jnp.dot(q_sq, jnp.ones((MLA_QK_PAD, LANES), BF16), preferred_element_type=F32)
    r_q = lax.rsqrt(ss_q * (1.0 / MLA_QK) + RMS_EPS)
    cos_t = cos_ref[0]
    sin_t = sin_ref[0]
    qn = qn_ref[...]
    for hd in range(heads):
        r = r_q[hd * tm:(hd + 1) * tm]
        t = q_tiles[hd]
        q_out[0, hd, :, :LANES] = (t[:, :LANES] * r * qn[:, :LANES]).astype(BF16)
        q_out[0, hd, :, LANES:] = _rope(t[:, LANES:] * r * qn[:, LANES:], cos_t, sin_t).astype(BF16)

    kr_t = k_rope.T
    ss_r = jnp.sum(kr_t * kr_t, axis=0, keepdims=True)
    kr_g = kr_t * kn_rope_ref[...]
    x1 = kr_g[0:quarter]
    x2 = kr_g[2 * quarter:3 * quarter]
    cos_tt = cost_ref[0]
    sin_tt = sint_ref[0]
    zero = jnp.zeros((quarter, tm), F32)
    kr_rot = jnp.concatenate([x1 * cos_tt - x2 * sin_tt, zero, x2 * cos_tt + x1 * sin_tt, zero], axis=0)
    kn_gain = kn_nope_ref[...]
    ones = jnp.ones((tm, LANES), BF16)
    for hd in range(heads):
        kn = kn_t[hd * LANES:(hd + 1) * LANES]
        ss_k = jnp.sum(kn * kn, axis=0, keepdims=True) + ss_r
        r_k = lax.rsqrt(ss_k * (1.0 / MLA_QK) + RMS_EPS)
        kt_out[0, hd, :LANES, :] = (kn * r_k * kn_gain).astype(BF16)
        kt_out[0, hd, LANES:, :] = (kr_rot * r_k).astype(BF16)
        v_out[0, hd, :, :MLA_V] = v_all[:, hd * MLA_V:(hd + 1) * MLA_V].astype(BF16)
        v_out[0, hd, :, MLA_V:] = ones


def _mla_in(x, cos_t, sin_t, cos_tt, sin_tt, gain, w_in, qa, w_q, kva, w_knt, w_v, qn, kn_nope, kn_rope,
            heads, tm):
    b, s, d = x.shape
    q_lora = qa.shape[1]
    kv_lora = kva.shape[1]
    const = lambda bi, si: (0, 0)
    row = lambda w: pl.BlockSpec((1, tm, w), lambda bi, si: (bi, si, 0))
    col = pl.BlockSpec((1, cos_tt.shape[1], tm), lambda bi, si: (bi, 0, si))
    head_out = lambda w: pl.BlockSpec((1, heads, tm, w), lambda bi, si: (bi, 0, si, 0))
    return pl.pallas_call(
        functools.partial(_mla_in_kernel, q_lora=q_lora, kv_lora=kv_lora),
        grid=(b, s // tm),
        in_specs=[
            row(d), row(LANES), row(LANES), col, col,
            _resident((1, d), const),
            _resident(w_in.shape, const),
            _resident(qa.shape, const),
            _resident(w_q.shape, const),
            _resident(kva.shape, const),
            _resident(w_knt.shape, const),
            _resident(w_v.shape, const),
            _resident(qn.shape, const),
            _resident(kn_nope.shape, const),
            _resident(kn_rope.shape, const),
        ],
        out_specs=[
            head_out(MLA_QK_PAD),
            pl.BlockSpec((1, heads, MLA_QK_PAD, tm), lambda bi, si: (bi, 0, 0, si)),
            head_out(MLA_V + LANES),
        ],
        out_shape=[
            jax.ShapeDtypeStruct((b, heads, s, MLA_QK_PAD), BF16),
            jax.ShapeDtypeStruct((b, heads, MLA_QK_PAD, s), BF16),
            jax.ShapeDtypeStruct((b, heads, s, MLA_V + LANES), BF16),
        ],
        compiler_params=_compiler_params(("arbitrary", "arbitrary")),
        name="mla_in",
    )(x, cos_t, sin_t, cos_tt, sin_tt, gain, w_in, qa, w_q, kva, w_knt, w_v, qn, kn_nope, kn_rope)


def _attn_online_kernel(q_ref, kt_ref, v_ref, o_ref, acc_ref, m_ref, alpha_ref, s0_ref, s1_ref, p_ref,
                        *, tq):
    qi = pl.program_id(2)
    hg = q_ref.shape[1]
    dv = v_ref.shape[-1] - LANES
    lane_tiles = tq // LANES

    def scores(ki, s_ref):
        cols = pl.ds(pl.multiple_of(ki * tq, tq), tq)
        for hd in range(hg):
            s_ref[hd] = jnp.dot(q_ref[0, hd], kt_ref[0, hd, :, cols], preferred_element_type=F32)

    def consume(ki, s_ref, masked):
        rows = pl.ds(pl.multiple_of(ki * tq, tq), tq)
        for hd in range(hg):
            for g in range(tq // ATTN_ROW_GROUP):
                rg = slice(g * ATTN_ROW_GROUP, (g + 1) * ATTN_ROW_GROUP)
                s = s_ref[hd, rg, :]
                if masked:
                    r = lax.broadcasted_iota(jnp.int32, s.shape, 0) + g * ATTN_ROW_GROUP
                    c = lax.broadcasted_iota(jnp.int32, s.shape, 1)
                    s = jnp.where(c <= r, s, NEG_BIG)
                m_prev = m_ref[hd, rg, :]
                m_new = jnp.maximum(m_prev, jnp.max(s, axis=-1, keepdims=True))
                alpha = jnp.exp2(m_prev - m_new)
                m_ref[hd, rg, :] = m_new
                alpha_ref[hd, rg, :LANES] = alpha
                alpha_ref[hd, rg, LANES:] = alpha
                m_wide = jnp.concatenate([m_new] * lane_tiles, axis=1)
                p_ref[hd, rg, :] = jnp.exp2(s - m_wide).astype(BF16)
            pv = jnp.dot(p_ref[hd], v_ref[0, hd, rows, :], preferred_element_type=F32)
            acc_ref[hd] = alpha_ref[hd] * acc_ref[hd] + pv

    def pair(kp, carry):
        k0 = 2 * kp
        scores(k0 + 1, s1_ref)
        consume(k0, s0_ref, False)
        scores(k0 + 2, s0_ref)
        consume(k0 + 1, s1_ref, False)
        return carry

    acc_ref[...] = jnp.zeros_like(acc_ref)
    m_ref[...] = jnp.full(m_ref.shape, NEG_BIG, F32)
    scores(0, s0_ref)
    lax.fori_loop(0, qi // 2, pair, 0)

    @pl.when(qi % 2 == 0)
    def _():
        consume(qi, s0_ref, True)

    @pl.when(qi % 2 == 1)
    def _():
        scores(qi, s1_ref)
        consume(qi - 1, s0_ref, False)
        consume(qi, s1_ref, True)

    for hd in range(hg):
        o_ref[0, :, hd * dv:(hd + 1) * dv] = (acc_ref[hd, :, :dv] / acc_ref[hd, :, dv:]).astype(o_ref.dtype)


def _attn_bounded_kernel(q_ref, kt_ref, v_ref, o_ref, acc_ref, p0_ref, p1_ref, pd_ref, *, tq):
    qi = pl.program_id(2)
    hg = q_ref.shape[1]
    dv = v_ref.shape[-1] - LANES

    def probs(ki, p_ref, masked):
        cols = pl.ds(pl.multiple_of(ki * tq, tq), tq)
        for hd in range(hg):
            s = jnp.dot(q_ref[0, hd], kt_ref[0, hd, :, cols], preferred_element_type=F32)
            if masked:
                r = lax.broadcasted_iota(jnp.int32, s.shape, 0)
                c = lax.broadcasted_iota(jnp.int32, s.shape, 1)
                s = jnp.where(c <= r, s, NEG_BIG)
            p_ref[hd] = jnp.exp2(s).astype(BF16)

    def accumulate(ki, p_ref):
        rows = pl.ds(pl.multiple_of(ki * tq, tq), tq)
        for hd in range(hg):
            acc_ref[hd] += jnp.dot(p_ref[hd], v_ref[0, hd, rows, :], preferred_element_type=F32)

    def pair(kp, carry):
        k0 = 2 * kp
        probs(k0 + 1, p1_ref, False)
        accumulate(k0, p0_ref)
        probs(k0 + 2, p0_ref, False)
        accumulate(k0 + 1, p1_ref)
        return carry

    acc_ref[...] = jnp.zeros_like(acc_ref)

    @pl.when(qi == 0)
    def _():
        probs(0, pd_ref, True)
        accumulate(0, pd_ref)

    @pl.when(qi > 0)
    def _():
        probs(0, p0_ref, False)
        lax.fori_loop(0, (qi - 1) // 2, pair, 0)

        @pl.when(qi % 2 == 1)
        def _():
            probs(qi, pd_ref, True)
            accumulate(qi - 1, p0_ref)
            accumulate(qi, pd_ref)

        @pl.when(qi % 2 == 0)
        def _():
            probs(qi - 1, p1_ref, False)
            accumulate(qi - 2, p0_ref)
            probs(qi, pd_ref, True)
            accumulate(qi - 1, p1_ref)
            accumulate(qi, pd_ref)

    for hd in range(hg):
        o_ref[0, :, hd * dv:(hd + 1) * dv] = (acc_ref[hd, :, :dv] / acc_ref[hd, :, dv:]).astype(o_ref.dtype)


def _attention(q, kt, v, tq, hg, bounded):
    b, heads, s, dq = q.shape
    dve = v.shape[-1]
    dv = dve - LANES
    acc = pltpu.VMEM((hg, tq, dve), F32)
    probs = pltpu.VMEM((hg, tq, tq), BF16)
    scores = pltpu.VMEM((hg, tq, tq), F32)
    if bounded:
        body, scratch = _attn_bounded_kernel, [acc, probs, probs, probs]
    else:
        running_max = pltpu.VMEM((hg, tq, LANES), F32)
        rescale = pltpu.VMEM((hg, tq, dve), F32)
        body, scratch = _attn_online_kernel, [acc, running_max, rescale, scores, scores, probs]
    return pl.pallas_call(
        functools.partial(body, tq=tq),
        grid=(b, heads // hg, s // tq),
        in_specs=[
            pl.BlockSpec((1, hg, tq, dq), lambda bi, hi, qi: (bi, hi, qi, 0)),
            pl.BlockSpec((1, hg, dq, s), lambda bi, hi, qi: (bi, hi, 0, 0)),
            pl.BlockSpec((1, hg, s, dve), lambda bi, hi, qi: (bi, hi, 0, 0)),
        ],
        out_specs=pl.BlockSpec((1, tq, hg * dv), lambda bi, hi, qi: (bi, qi, hi)),
        out_shape=jax.ShapeDtypeStruct((b, s, heads * dv), BF16),
        scratch_shapes=scratch,
        compiler_params=_compiler_params(("arbitrary", "arbitrary", "arbitrary")),
        name="mla_attention_bounded" if bounded else "mla_attention_online",
    )(q, kt, v)


def _ffn_kernel(x_ref, a_ref, w_o_ref, g_ref, w_up_ref, cw_ref, cb_ref, w_down_ref, o_ref, tail_ref, shift_ref,
                *, d_ff, fc):
    @pl.when(pl.program_id(1) == 0)
    def _():
        tail_ref[...] = jnp.zeros_like(tail_ref)

    x1 = x_ref[0] + jnp.dot(a_ref[0], w_o_ref[...], preferred_element_type=F32)
    tm, d = x1.shape
    h = (x1 * _rms_scale(x1, d) * g_ref[...]).astype(BF16)

    def conv(u, col0):
        cols = slice(col0, col0 + fc)
        shift_ref[SUBLANES:SUBLANES + tm, :] = u
        shift_ref[0:SUBLANES, :] = tail_ref[:, cols]
        u1 = shift_ref[SUBLANES - 1:SUBLANES - 1 + tm, :]
        u2 = shift_ref[SUBLANES - 2:SUBLANES - 2 + tm, :]
        tail_ref[:, cols] = u[tm - SUBLANES:tm, :]
        w = cw_ref[:, cols]
        return cb_ref[:, cols] + u2 * w[0:1] + u1 * w[1:2] + u * w[2:3]

    acc = x1
    for j in range(d_ff // fc):
        c0 = j * fc
        u_gate = jnp.dot(h, w_up_ref[:, c0:c0 + fc], preferred_element_type=F32)
        gate = conv(u_gate, c0)
        u_up = jnp.dot(h, w_up_ref[:, d_ff + c0:d_ff + c0 + fc], preferred_element_type=F32)
        up = conv(u_up, d_ff + c0)
        act = (gate * _sigmoid(gate) * up).astype(BF16)
        acc = acc + jnp.dot(act, w_down_ref[c0:c0 + fc, :], preferred_element_type=F32)
    o_ref[0] = acc


def _ffn(x, a, w_o, gain, w_up, conv_w, conv_b, w_down, tm, fc):
    b, s, d = x.shape
    d_ff = w_down.shape[0]
    const = lambda bi, si: (0, 0)
    row = pl.BlockSpec((1, tm, d), lambda bi, si: (bi, si, 0))
    return pl.pallas_call(
        functools.partial(_ffn_kernel, d_ff=d_ff, fc=fc),
        grid=(b, s // tm),
        in_specs=[
            row,
            pl.BlockSpec((1, tm, a.shape[-1]), lambda bi, si: (bi, si, 0)),
            _resident(w_o.shape, const),
            _resident((1, d), const),
            _resident(w_up.shape, const),
            _resident(conv_w.shape, const),
            _resident(conv_b.shape, const),
            _resident(w_down.shape, const),
        ],
        out_specs=row,
        out_shape=jax.ShapeDtypeStruct((b, s, d), F32),
        scratch_shapes=[
            pltpu.VMEM((SUBLANES, 2 * d_ff), F32),
            pltpu.VMEM((SUBLANES + tm, fc), F32),
        ],
        compiler_params=_compiler_params(("arbitrary", "arbitrary")),
        name="conv_ffn",
    )(x, a, w_o, gain, w_up, conv_w, conv_b, w_down)


def _spread_rope(w):
    half = MLA_ROPE // 2
    z = jnp.zeros(w.shape[:-1] + (LANES // 2 - half,), w.dtype)
    return jnp.concatenate([w[..., :half], z, w[..., half:], z], axis=-1)


def _pad_qk_channels(w, heads):
    w = w.reshape(w.shape[:-1] + (heads, MLA_QK))
    w = jnp.concatenate([w[..., :MLA_NOPE], _spread_rope(w[..., MLA_NOPE:])], axis=-1)
    return w.reshape(w.shape[:-2] + (heads * MLA_QK_PAD,))


def kernel(x, positions, norm_mix, norm_ffn, hgrn_w_in, hgrn_lower_bounds, hgrn_out_norm, hgrn_w_out,
           mla_w_in, mla_q_a_norm, mla_w_q_up, mla_kv_a_norm, mla_w_kv_up, mla_q_norm, mla_k_norm,
           mla_w_out, ffn_w_up, ffn_conv_w, ffn_conv_b, ffn_w_down):
    b, s, d = x.shape
    depth = norm_mix.shape[0]
    n_mixers = 2
    q_lora = mla_q_a_norm.shape[1]
    kv_lora = mla_kv_a_norm.shape[1]
    heads = mla_w_q_up.shape[2] // MLA_QK
    mla_tm = 512

    lb_soft = jax.nn.softmax(hgrn_lower_bounds.astype(F32), axis=0)
    lower_bounds = jnp.cumsum(lb_soft, axis=0) - lb_soft[0:1]
    inv_freq = ROPE_THETA ** (-jnp.arange(0, MLA_ROPE, 2, dtype=F32) / MLA_ROPE)
    ang = positions.astype(F32)[..., None] * inv_freq
    cos, sin = jnp.cos(ang), jnp.sin(ang)
    zeros = jnp.zeros_like(cos)
    cos_t = jnp.concatenate([cos, zeros, cos, zeros], axis=-1)
    sin_t = jnp.concatenate([-sin, zeros, sin, zeros], axis=-1)
    cos_tt = jnp.swapaxes(cos, 1, 2)
    sin_tt = jnp.swapaxes(sin, 1, 2)

    x = x.astype(F32)
    for layer in range(depth):
        j = layer // n_mixers
        gain = norm_mix[layer].reshape(1, d).astype(F32)
        if layer % n_mixers == 0:
            q, lf, k, v, gate = _hgrn_in(
                x.reshape(b * s, d), gain, hgrn_w_in[j].astype(BF16),
                lower_bounds[j].reshape(1, d), tm=512)
            shp = (b, s, d)
            o = _hgrn_rec(q.reshape(shp), lf.reshape(shp), k.reshape(shp), v.reshape(shp),
                          gate.reshape(shp), hgrn_out_norm[j].reshape(1, HGRN_HEAD).astype(F32), tt=512)
            w_out = hgrn_w_out[j]
        else:
            w_in = jnp.concatenate(
                [mla_w_in[j][:, :q_lora + kv_lora], _spread_rope(mla_w_in[j][:, q_lora + kv_lora:])], axis=-1)
            w_kv = mla_w_kv_up[j].reshape(kv_lora, heads, MLA_NOPE + MLA_V)
            w_knt = jnp.transpose(w_kv[..., :MLA_NOPE], (1, 2, 0)).reshape(heads * MLA_NOPE, kv_lora)
            w_v = w_kv[..., MLA_NOPE:].reshape(kv_lora, heads * MLA_V)
            q_gain = _pad_qk_channels(mla_q_norm[j].reshape(1, MLA_QK), 1).astype(F32) * MLA_Q_SCALE
            k_gain = _pad_qk_channels(mla_k_norm[j].reshape(1, MLA_QK), 1).astype(F32)
            k_gain_t = jnp.broadcast_to(k_gain.reshape(MLA_QK_PAD, 1), (MLA_QK_PAD, mla_tm))
            bound = 1.02 * MLA_QK * MLA_Q_SCALE * jnp.max(jnp.abs(mla_q_norm[j])) * jnp.max(jnp.abs(mla_k_norm[j]))
            qh, kth, vh = _mla_in(
                x, cos_t, sin_t, cos_tt, sin_tt, gain, w_in.astype(BF16),
                mla_q_a_norm[j].reshape(1, q_lora).astype(F32),
                _pad_qk_channels(mla_w_q_up[j], heads).astype(BF16),
                mla_kv_a_norm[j].reshape(1, kv_lora).astype(F32),
                w_knt.astype(BF16), w_v.astype(BF16),
                q_gain, k_gain_t[:LANES], k_gain_t[LANES:],
                heads=heads, tm=mla_tm)
            o = lax.cond(
                bound <= MLA_SCORE_MAX,
                functools.partial(_attention, tq=512, hg=2, bounded=True),
                functools.partial(_attention, tq=512, hg=2, bounded=False),
                qh, kth, vh)
            w_out = mla_w_out[j]
        x = _ffn(x, o, w_out.astype(BF16), norm_ffn[layer].reshape(1, d).astype(F32),
                 ffn_w_up[layer].astype(BF16), ffn_conv_w[layer].astype(F32),
                 ffn_conv_b[layer].reshape(1, -1).astype(F32), ffn_w_down[layer].astype(BF16),
                 tm=512, fc=256)
    return x
```

```python
import functools
import math

import jax
import jax.numpy as jnp
from jax import lax
from jax.experimental import pallas as pl
from jax.experimental.pallas import tpu as pltpu

F32 = jnp.float32
BF16 = jnp.bfloat16

RMS_EPS = 1e-6
ROPE_THETA = 10000.0

LANES = 128
SUBLANES = 8
VMEM_LIMIT_BYTES = 56 * 1024 * 1024

HGRN_HEAD = 128
HGRN_CHUNK = 64
HGRN_SUB = 16
MLA_NOPE = 128
MLA_ROPE = 64
MLA_QK = MLA_NOPE + MLA_ROPE
MLA_V = 128
MLA_QK_PAD = 2 * LANES
FFN_CONV = 3
ATTN_ROW_GROUP = 64
NEG_BIG = -1e30
MLA_Q_SCALE = (MLA_QK ** -0.5) * math.log2(math.e)
MLA_SCORE_MAX = 60.0

_NT = (((1,), (1,)), ((), ()))


def _compiler_params(semantics):
    return pltpu.CompilerParams(dimension_semantics=semantics, vmem_limit_bytes=VMEM_LIMIT_BYTES)


def _resident(shape, index_map):
    return pl.BlockSpec(shape, index_map, pipeline_mode=pl.Buffered(1))


def _rms_scale(x, n):
    return lax.rsqrt(jnp.sum(x * x, axis=-1, keepdims=True) * (1.0 / n) + RMS_EPS)


def _sigmoid(x):
    return 1.0 / (1.0 + jnp.exp(-x))


def _hgrn_in_kernel(x_ref, g_ref, w_ref, lb_ref, q_ref, lf_ref, v_ref, gate_ref):
    x = x_ref[...]
    d = x.shape[-1]
    h = (x * _rms_scale(x, d) * g_ref[...]).astype(BF16)
    p = jnp.dot(h, w_ref[...], preferred_element_type=F32)
    q = p[:, 0 * d:1 * d]
    f = p[:, 1 * d:2 * d]
    lb = lb_ref[...]
    forget = lb + (1.0 - lb) * _sigmoid(f)
    q_ref[...] = q * _sigmoid(q)
    lf_ref[...] = jnp.log2(forget)
    v_ref[...] = p[:, 2 * d:3 * d].astype(BF16)
    g = p[:, 3 * d:4 * d]
    gate_ref[...] = g * _sigmoid(g)


def _hgrn_in(x2d, gain, w_in, lb, tm):
    m, d = x2d.shape
    row = pl.BlockSpec((tm, d), lambda i: (i, 0))
    vec = _resident((1, d), lambda i: (0, 0))
    return pl.pallas_call(
        _hgrn_in_kernel,
        grid=(m // tm,),
        in_specs=[row, vec, _resident((d, 4 * d), lambda i: (0, 0)), vec],
        out_specs=[row, row, row, row],
        out_shape=[
            jax.ShapeDtypeStruct((m, d), F32),
            jax.ShapeDtypeStruct((m, d), F32),
            jax.ShapeDtypeStruct((m, d), BF16),
            jax.ShapeDtypeStruct((m, d), F32),
        ],
        compiler_params=_compiler_params(("arbitrary",)),
        name="hgrn_in",
    )(x2d, gain, w_in, lb)


def _cumsum_rows(x):
    c = x.shape[0]
    row = lax.broadcasted_iota(jnp.int32, (c, c), 0)
    col = lax.broadcasted_iota(jnp.int32, (c, c), 1)
    tri = jnp.where(col <= row, 1.0, 0.0).astype(BF16)
    hi = x.astype(BF16)
    rem = x - hi.astype(F32)
    mid = rem.astype(BF16)
    lo = (rem - mid.astype(F32)).astype(BF16)
    out = jnp.dot(tri, hi, preferred_element_type=F32)
    out = out + jnp.dot(tri, mid, preferred_element_type=F32)
    return out + jnp.dot(tri, lo, preferred_element_type=F32)


def _hgrn_chunk(q, lf, v, gate, gain, state_t):
    c = HGRN_CHUNK
    sub = HGRN_SUB
    half = SUBLANES
    k = 1.0 - jnp.exp2(lf)
    g_cum = _cumsum_rows(lf)
    g_last = g_cum[c - 1:c, :]

    q_dec = (q * jnp.exp2(g_cum)).astype(BF16)
    o = lax.dot_general(q_dec, state_t.astype(BF16), _NT, preferred_element_type=F32)

    k_dec = (k * jnp.exp2(g_last - g_cum)).astype(BF16)
    v_t = v.astype(F32).T.astype(BF16)
    new_state_t = state_t * jnp.exp2(g_last) + jnp.dot(v_t, k_dec, preferred_element_type=F32)

    lane = lax.broadcasted_iota(jnp.int32, (half, c), 1)
    sub_row = lax.broadcasted_iota(jnp.int32, (half, HGRN_HEAD), 0)
    blocks = []
    for j in range(c // sub):
        r0 = j * sub
        qb = q[r0:r0 + sub]
        kb = k[r0:r0 + sub]
        gb = g_cum[r0:r0 + sub]
        if j == 0:
            a_top = jnp.zeros((half, c), F32)
            a_bot = jnp.zeros((half, c), F32)
        else:
            g_ref = g_cum[r0 - 1:r0, :]
            q_off = (qb * jnp.exp2(gb - g_ref)).astype(BF16)
            k_off = k[:r0] * jnp.exp2(jnp.minimum(g_ref - g_cum[:r0], 0.0))
            k_off = jnp.concatenate([k_off, jnp.zeros((c - r0, HGRN_HEAD), F32)], axis=0).astype(BF16)
            a_off = lax.dot_general(q_off, k_off, _NT, preferred_element_type=F32)
            a_top = a_off[:half]
            a_bot = a_off[half:]
        q_top, q_bot = qb[:half], qb[half:]
        g_top, g_bot = gb[:half], gb[half:]
        for s in range(sub):
            gs = gb[s:s + 1]
            ks = kb[s:s + 1]
            if s < half:
                e_top = jnp.exp2(jnp.where(sub_row >= s, g_top - gs, NEG_BIG))
                col_top = jnp.sum(q_top * (ks * e_top), axis=-1, keepdims=True)
                a_top = jnp.where(lane == r0 + s, col_top, a_top)
                e_bot = jnp.exp2(g_bot - gs)
            else:
                e_bot = jnp.exp2(jnp.where(sub_row >= s - half, g_bot - gs, NEG_BIG))
            col_bot = jnp.sum(q_bot * (ks * e_bot), axis=-1, keepdims=True)
            a_bot = jnp.where(lane == r0 + s, col_bot, a_bot)
        blocks += [a_top, a_bot]
    a = jnp.concatenate(blocks, axis=0).astype(BF16)
    o = o + jnp.dot(a, v, preferred_element_type=F32)

    y = o * _rms_scale(o, HGRN_HEAD) * gain
    return (y * gate).astype(BF16), new_state_t


def _hgrn_rec_kernel(q_ref, lf_ref, v_ref, gate_ref, gain_ref, o_ref, state_ref):
    @pl.when(pl.program_id(2) == 0)
    def _():
        state_ref[...] = jnp.zeros_like(state_ref)

    gain = gain_ref[...]
    for hd in range(q_ref.shape[2] // HGRN_HEAD):
        cols = slice(hd * HGRN_HEAD, (hd + 1) * HGRN_HEAD)
        state_t = state_ref[hd]
        for ci in range(q_ref.shape[1] // HGRN_CHUNK):
            rows = slice(ci * HGRN_CHUNK, (ci + 1) * HGRN_CHUNK)
            o_ref[0, rows, cols], state_t = _hgrn_chunk(
                q_ref[0, rows, cols], lf_ref[0, rows, cols], v_ref[0, rows, cols], gate_ref[0, rows, cols],
                gain, state_t)
        state_ref[hd] = state_t


def _hgrn_rec(q, lf, v, gate, gain, tt, hg):
    b, s, d = q.shape
    heads = d // HGRN_HEAD
    blk = pl.BlockSpec((1, tt, hg * HGRN_HEAD), lambda bi, hi, ti: (bi, ti, hi))
    return pl.pallas_call(
        _hgrn_rec_kernel,
        grid=(b, heads // hg, s // tt),
        in_specs=[blk, blk, blk, blk, pl.BlockSpec((1, HGRN_HEAD), lambda bi, hi, ti: (0, 0))],
        out_specs=blk,
        out_shape=jax.ShapeDtypeStruct((b, s, d), BF16),
        scratch_shapes=[pltpu.VMEM((hg, HGRN_HEAD, HGRN_HEAD), F32)],
        compiler_params=_compiler_params(("arbitrary", "arbitrary", "arbitrary")),
        name="hgrn_rec",
    )(q, lf, v, gate, gain)


def _rope(t, cos_t, sin_t):
    return t * cos_t + pltpu.roll(t, LANES // 2, axis=1) * sin_t


def _mla_in_kernel(x_ref, cos_ref, sin_ref, cost_ref, sint_ref, g_ref, w_in_ref, qa_ref, w_q_ref, kva_ref,
                   w_knt_ref, w_v_ref, qn_ref, kn_nope_ref, kn_rope_ref, q_out, kt_out, v_out,
                   *, q_lora, kv_lora):
    x = x_ref[0]
    tm, d = x.shape
    heads = q_out.shape[1]
    quarter = LANES // 4
    h = (x * _rms_scale(x, d) * g_ref[...]).astype(BF16)
    c = jnp.dot(h, w_in_ref[...], preferred_element_type=F32)
    c_q = c[:, :q_lora]
    c_kv = c[:, q_lora:q_lora + kv_lora]
    k_rope = c[:, q_lora + kv_lora:]
    c_q = (c_q * _rms_scale(c_q, q_lora) * qa_ref[...]).astype(BF16)
    c_kv = (c_kv * _rms_scale(c_kv, kv_lora) * kva_ref[...]).astype(BF16)
    q = jnp.dot(c_q, w_q_ref[...], preferred_element_type=F32)
    v_all = jnp.dot(c_kv, w_v_ref[...], preferred_element_type=F32)
    kn_t = lax.dot_general(w_knt_ref[...], c_kv, _NT, preferred_element_type=F32)

    q_tiles = [q[:, hd * MLA_QK_PAD:(hd + 1) * MLA_QK_PAD] for hd in range(heads)]
    q_sq = jnp.concatenate([(t * t).astype(BF16) for t in q_tiles], axis=0)
    ss_q = jnp.dot(q_sq, jnp.ones((MLA_QK_PAD, LANES), BF16), preferred_element_type=F32)
    r_q = lax.rsqrt(ss_q * (1.0 / MLA_QK) + RMS_EPS)
    cos_t = cos_ref[0]
    sin_t = sin_ref[0]
    qn = qn_ref[...]
    for hd in range(heads):
        r = r_q[hd * tm:(hd + 1) * tm]
        t = q_tiles[hd]
        q_out[0, hd, :, :LANES] = (t[:, :LANES] * r * qn[:, :LANES]).astype(BF16)
        q_out[0, hd, :, LANES:] = _rope(t[:, LANES:] * r * qn[:, LANES:], cos_t, sin_t).astype(BF16)

    kr_t = k_rope.T
    ss_r = jnp.sum(kr_t * kr_t, axis=0, keepdims=True)
    kr_g = kr_t * kn_rope_ref[...]
    x1 = kr_g[0:quarter]
    x2 = kr_g[2 * quarter:3 * quarter]
    cos_tt = cost_ref[0]
    sin_tt = sint_ref[0]
    zero = jnp.zeros((quarter, tm), F32)
    kr_rot = jnp.concatenate([x1 * cos_tt - x2 * sin_tt, zero, x2 * cos_tt + x1 * sin_tt, zero], axis=0)
    kn_gain = kn_nope_ref[...]
    ones = jnp.ones((tm, LANES), BF16)
    for hd in range(heads):
        kn = kn_t[hd * LANES:(hd + 1) * LANES]
        ss_k = jnp.sum(kn * kn, axis=0, keepdims=True) + ss_r
        r_k = lax.rsqrt(ss_k * (1.0 / MLA_QK) + RMS_EPS)
        kt_out[0, hd, :LANES, :] = (kn * r_k * kn_gain).astype(BF16)
        kt_out[0, hd, LANES:, :] = (kr_rot * r_k).astype(BF16)
        v_out[0, hd, :, :MLA_V] = v_all[:, hd * MLA_V:(hd + 1) * MLA_V].astype(BF16)
        v_out[0, hd, :, MLA_V:] = ones


def _mla_in(x, cos_t, sin_t, cos_tt, sin_tt, gain, w_in, qa, w_q, kva, w_knt, w_v, qn, kn_nope, kn_rope,
            heads, tm):
    b, s, d = x.shape
    q_lora = qa.shape[1]
    kv_lora = kva.shape[1]
    const = lambda bi, si: (0, 0)
    row = lambda w: pl.BlockSpec((1, tm, w), lambda bi, si: (bi, si, 0))
    col = pl.BlockSpec((1, cos_tt.shape[1], tm), lambda bi, si: (bi, 0, si))
    head_out = lambda w: pl.BlockSpec((1, heads, tm, w), lambda bi, si: (bi, 0, si, 0))
    return pl.pallas_call(
        functools.partial(_mla_in_kernel, q_lora=q_lora, kv_lora=kv_lora),
        grid=(b, s // tm),
        in_specs=[
            row(d), row(LANES), row(LANES), col, col,
            _resident((1, d), const),
            _resident(w_in.shape, const),
            _resident(qa.shape, const),
            _resident(w_q.shape, const),
            _resident(kva.shape, const),
            _resident(w_knt.shape, const),
            _resident(w_v.shape, const),
            _resident(qn.shape, const),
            _resident(kn_nope.shape, const),
            _resident(kn_rope.shape, const),
        ],
        out_specs=[
            head_out(MLA_QK_PAD),
            pl.BlockSpec((1, heads, MLA_QK_PAD, tm), lambda bi, si: (bi, 0, 0, si)),
            head_out(MLA_V + LANES),
        ],
        out_shape=[
            jax.ShapeDtypeStruct((b, heads, s, MLA_QK_PAD), BF16),
            jax.ShapeDtypeStruct((b, heads, MLA_QK_PAD, s), BF16),
            jax.ShapeDtypeStruct((b, heads, s, MLA_V + LANES), BF16),
        ],
        compiler_params=_compiler_params(("arbitrary", "arbitrary")),
        name="mla_in",
    )(x, cos_t, sin_t, cos_tt, sin_tt, gain, w_in, qa, w_q, kva, w_knt, w_v, qn, kn_nope, kn_rope)


def _attn_online_kernel(q_ref, kt_ref, v_ref, o_ref, acc_ref, m_ref, alpha_ref, s0_ref, s1_ref, p_ref,
                        *, tq):
    qi = pl.program_id(2)
    hg = q_ref.shape[1]
    dv = v_ref.shape[-1] - LANES
    lane_tiles = tq // LANES

    def scores(ki, s_ref):
        cols = pl.ds(pl.multiple_of(ki * tq, tq), tq)
        for hd in range(hg):
            s_ref[hd] = jnp.dot(q_ref[0, hd], kt_ref[0, hd, :, cols], preferred_element_type=F32)

    def consume(ki, s_ref, masked):
        rows = pl.ds(pl.multiple_of(ki * tq, tq), tq)
        for hd in range(hg):
            for g in range(tq // ATTN_ROW_GROUP):
                rg = slice(g * ATTN_ROW_GROUP, (g + 1) * ATTN_ROW_GROUP)
                s = s_ref[hd, rg, :]
                if masked:
                    r = lax.broadcasted_iota(jnp.int32, s.shape, 0) + g * ATTN_ROW_GROUP
                    c = lax.broadcasted_iota(jnp.int32, s.shape, 1)
                    s = jnp.where(c <= r, s, NEG_BIG)
                m_prev = m_ref[hd, rg, :]
                m_new = jnp.maximum(m_prev, jnp.max(s, axis=-1, keepdims=True))
                alpha = jnp.exp2(m_prev - m_new)
                m_ref[hd, rg, :] = m_new
                alpha_ref[hd, rg, :LANES] = alpha
                alpha_ref[hd, rg, LANES:] = alpha
                m_wide = jnp.concatenate([m_new] * lane_tiles, axis=1)
                p_ref[hd, rg, :] = jnp.exp2(s - m_wide).astype(BF16)
            pv = jnp.dot(p_ref[hd], v_ref[0, hd, rows, :], preferred_element_type=F32)
            acc_ref[hd] = alpha_ref[hd] * acc_ref[hd] + pv

    def pair(kp, carry):
        k0 = 2 * kp
        scores(k0 + 1, s1_ref)
        consume(k0, s0_ref, False)
        scores(k0 + 2, s0_ref)
        consume(k0 + 1, s1_ref, False)
        return carry

    acc_ref[...] = jnp.zeros_like(acc_ref)
    m_ref[...] = jnp.full(m_ref.shape, NEG_BIG, F32)
    scores(0, s0_ref)
    lax.fori_loop(0, qi // 2, pair, 0)

    @pl.when(qi % 2 == 0)
    def _():
        consume(qi, s0_ref, True)

    @pl.when(qi % 2 == 1)
    def _():
        scores(qi, s1_ref)
        consume(qi - 1, s0_ref, False)
        consume(qi, s1_ref, True)

    for hd in range(hg):
        o_ref[0, :, hd * dv:(hd + 1) * dv] = (acc_ref[hd, :, :dv] / acc_ref[hd, :, dv:]).astype(o_ref.dtype)


def _attn_bounded_kernel(q_ref, kt_ref, v_ref, o_ref, acc_ref, p0_ref, p1_ref, pd_ref, *, tq):
    qi = pl.program_id(2)
    hg = q_ref.shape[1]
    dv = v_ref.shape[-1] - LANES

    def probs(ki, p_ref, masked):
        cols = pl.ds(pl.multiple_of(ki * tq, tq), tq)
        for hd in range(hg):
            s = jnp.dot(q_ref[0, hd], kt_ref[0, hd, :, cols], preferred_element_type=F32)
            if masked:
                r = lax.broadcasted_iota(jnp.int32, s.shape, 0)
                c = lax.broadcasted_iota(jnp.int32, s.shape, 1)
                s = jnp.where(c <= r, s, NEG_BIG)
            p_ref[hd] = jnp.exp2(s).astype(BF16)

    def accumulate(ki, p_ref):
        rows = pl.ds(pl.multiple_of(ki * tq, tq), tq)
        for hd in range(hg):
            acc_ref[hd] += jnp.dot(p_ref[hd], v_ref[0, hd, rows, :], preferred_element_type=F32)

    def pair(kp, carry):
        k0 = 2 * kp
        probs(k0 + 1, p1_ref, False)
        accumulate(k0, p0_ref)
        probs(k0 + 2, p0_ref, False)
        accumulate(k0 + 1, p1_ref)
        return carry

    acc_ref[...] = jnp.zeros_like(acc_ref)

    @pl.when(qi == 0)
    def _():
        probs(0, pd_ref, True)
        accumulate(0, pd_ref)

    @pl.when(qi > 0)
    def _():
        probs(0, p0_ref, False)
        lax.fori_loop(0, (qi - 1) // 2, pair, 0)

        @pl.when(qi % 2 == 1)
        def _():
            probs(qi, pd_ref, True)
            accumulate(qi - 1, p0_ref)
            accumulate(qi, pd_ref)

        @pl.when(qi % 2 == 0)
        def _():
            probs(qi - 1, p1_ref, False)
            accumulate(qi - 2, p0_ref)
            probs(qi, pd_ref, True)
            accumulate(qi - 1, p1_ref)
            accumulate(qi, pd_ref)

    for hd in range(hg):
        o_ref[0, :, hd * dv:(hd + 1) * dv] = (acc_ref[hd, :, :dv] / acc_ref[hd, :, dv:]).astype(o_ref.dtype)


def _attention(q, kt, v, tq, hg, bounded):
    b, heads, s, dq = q.shape
    dve = v.shape[-1]
    dv = dve - LANES
    acc = pltpu.VMEM((hg, tq, dve), F32)
    probs = pltpu.VMEM((hg, tq, tq), BF16)
    scores = pltpu.VMEM((hg, tq, tq), F32)
    if bounded:
        body, scratch = _attn_bounded_kernel, [acc, probs, probs, probs]
    else:
        running_max = pltpu.VMEM((hg, tq, LANES), F32)
        rescale = pltpu.VMEM((hg, tq, dve), F32)
        body, scratch = _attn_online_kernel, [acc, running_max, rescale, scores, scores, probs]
    return pl.pallas_call(
        functools.partial(body, tq=tq),
        grid=(b, heads // hg, s // tq),
        in_specs=[
            pl.BlockSpec((1, hg, tq, dq), lambda bi, hi, qi: (bi, hi, qi, 0)),
            pl.BlockSpec((1, hg, dq, s), lambda bi, hi, qi: (bi, hi, 0, 0)),
            pl.BlockSpec((1, hg, s, dve), lambda bi, hi, qi: (bi, hi, 0, 0)),
        ],
        out_specs=pl.BlockSpec((1, tq, hg * dv), lambda bi, hi, qi: (bi, qi, hi)),
        out_shape=jax.ShapeDtypeStruct((b, s, heads * dv), BF16),
        scratch_shapes=scratch,
        compiler_params=_compiler_params(("arbitrary", "arbitrary", "arbitrary")),
        name="mla_attention_bounded" if bounded else "mla_attention_online",
    )(q, kt, v)


def _ffn_kernel(x_ref, a_ref, w_o_ref, g_ref, w_up_ref, cw_ref, cb_ref, w_down_ref, o_ref, tail_ref, shift_ref,
                *, d_ff, fc):
    @pl.when(pl.program_id(1) == 0)
    def _():
        tail_ref[...] = jnp.zeros_like(tail_ref)

    x1 = x_ref[0] + jnp.dot(a_ref[0], w_o_ref[...], preferred_element_type=F32)
    tm, d = x1.shape
    h = (x1 * _rms_scale(x1, d) * g_ref[...]).astype(BF16)

    def conv(u, col0):
        cols = slice(col0, col0 + fc)
        shift_ref[SUBLANES:SUBLANES + tm, :] = u
        shift_ref[0:SUBLANES, :] = tail_ref[:, cols]
        u1 = shift_ref[SUBLANES - 1:SUBLANES - 1 + tm, :]
        u2 = shift_ref[SUBLANES - 2:SUBLANES - 2 + tm, :]
        tail_ref[:, cols] = u[tm - SUBLANES:tm, :]
        w = cw_ref[:, cols]
        return cb_ref[:, cols] + u2 * w[0:1] + u1 * w[1:2] + u * w[2:3]

    acc = x1
    for j in range(d_ff // fc):
        c0 = j * fc
        u_gate = jnp.dot(h, w_up_ref[:, c0:c0 + fc], preferred_element_type=F32)
        gate = conv(u_gate, c0)
        u_up = jnp.dot(h, w_up_ref[:, d_ff + c0:d_ff + c0 + fc], preferred_element_type=F32)
        up = conv(u_up, d_ff + c0)
        act = (gate * _sigmoid(gate) * up).astype(BF16)
        acc = acc + jnp.dot(act, w_down_ref[c0:c0 + fc, :], preferred_element_type=F32)
    o_ref[0] = acc


def _ffn(x, a, w_o, gain, w_up, conv_w, conv_b, w_down, tm, fc):
    b, s, d = x.shape
    d_ff = w_down.shape[0]
    const = lambda bi, si: (0, 0)
    row = pl.BlockSpec((1, tm, d), lambda bi, si: (bi, si, 0))
    return pl.pallas_call(
        functools.partial(_ffn_kernel, d_ff=d_ff, fc=fc),
        grid=(b, s // tm),
        in_specs=[
            row,
            pl.BlockSpec((1, tm, a.shape[-1]), lambda bi, si: (bi, si, 0)),
            _resident(w_o.shape, const),
            _resident((1, d), const),
            _resident(w_up.shape, const),
            _resident(conv_w.shape, const),
            _resident(conv_b.shape, const),
            _resident(w_down.shape, const),
        ],
        out_specs=row,
        out_shape=jax.ShapeDtypeStruct((b, s, d), F32),
        scratch_shapes=[
            pltpu.VMEM((SUBLANES, 2 * d_ff), F32),
            pltpu.VMEM((SUBLANES + tm, fc), F32),
        ],
        compiler_params=_compiler_params(("arbitrary", "arbitrary")),
        name="conv_ffn",
    )(x, a, w_o, gain, w_up, conv_w, conv_b, w_down)


def _spread_rope(w):
    half = MLA_ROPE // 2
    z = jnp.zeros(w.shape[:-1] + (LANES // 2 - half,), w.dtype)
    return jnp.concatenate([w[..., :half], z, w[..., half:], z], axis=-1)


def _pad_qk_channels(w, heads):
    w = w.reshape(w.shape[:-1] + (heads, MLA_QK))
    w = jnp.concatenate([w[..., :MLA_NOPE], _spread_rope(w[..., MLA_NOPE:])], axis=-1)
    return w.reshape(w.shape[:-2] + (heads * MLA_QK_PAD,))


def kernel(x, positions, norm_mix, norm_ffn, hgrn_w_in, hgrn_lower_bounds, hgrn_out_norm, hgrn_w_out,
           mla_w_in, mla_q_a_norm, mla_w_q_up, mla_kv_a_norm, mla_w_kv_up, mla_q_norm, mla_k_norm,
           mla_w_out, ffn_w_up, ffn_conv_w, ffn_conv_b, ffn_w_down):
    b, s, d = x.shape
    depth = norm_mix.shape[0]
    n_mixers = 2
    q_lora = mla_q_a_norm.shape[1]
    kv_lora = mla_kv_a_norm.shape[1]
    heads = mla_w_q_up.shape[2] // MLA_QK
    mla_tm = 512

    lb_soft = jax.nn.softmax(hgrn_lower_bounds.astype(F32), axis=0)
    lower_bounds = jnp.cumsum(lb_soft, axis=0) - lb_soft[0:1]
    inv_freq = ROPE_THETA ** (-jnp.arange(0, MLA_ROPE, 2, dtype=F32) / MLA_ROPE)
    ang = positions.astype(F32)[..., None] * inv_freq
    cos, sin = jnp.cos(ang), jnp.sin(ang)
    zeros = jnp.zeros_like(cos)
    cos_t = jnp.concatenate([cos, zeros, cos, zeros], axis=-1)
    sin_t = jnp.concatenate([-sin, zeros, sin, zeros], axis=-1)
    cos_tt = jnp.swapaxes(cos, 1, 2)
    sin_tt = jnp.swapaxes(sin, 1, 2)

    x = x.astype(F32)
    for layer in range(depth):
        j = layer // n_mixers
        gain = norm_mix[layer].reshape(1, d).astype(F32)
        if layer % n_mixers == 0:
            q, lf, v, gate = _hgrn_in(
                x.reshape(b * s, d), gain, hgrn_w_in[j].astype(BF16),
                lower_bounds[j].reshape(1, d), tm=512)
            shp = (b, s, d)
            o = _hgrn_rec(q.reshape(shp), lf.reshape(shp), v.reshape(shp), gate.reshape(shp),
                          hgrn_out_norm[j].reshape(1, HGRN_HEAD).astype(F32), tt=512, hg=2)
            w_out = hgrn_w_out[j]
        else:
            w_in = jnp.concatenate(
                [mla_w_in[j][:, :q_lora + kv_lora], _spread_rope(mla_w_in[j][:, q_lora + kv_lora:])], axis=-1)
            w_kv = mla_w_kv_up[j].reshape(kv_lora, heads, MLA_NOPE + MLA_V)
            w_knt = jnp.transpose(w_kv[..., :MLA_NOPE], (1, 2, 0)).reshape(heads * MLA_NOPE, kv_lora)
            w_v = w_kv[..., MLA_NOPE:].reshape(kv_lora, heads * MLA_V)
            q_gain = _pad_qk_channels(mla_q_norm[j].reshape(1, MLA_QK), 1).astype(F32) * MLA_Q_SCALE
            k_gain = _pad_qk_channels(mla_k_norm[j].reshape(1, MLA_QK), 1).astype(F32)
            k_gain_t = jnp.broadcast_to(k_gain.reshape(MLA_QK_PAD, 1), (MLA_QK_PAD, mla_tm))
            bound = 1.02 * MLA_QK * MLA_Q_SCALE * jnp.max(jnp.abs(mla_q_norm[j])) * jnp.max(jnp.abs(mla_k_norm[j]))
            qh, kth, vh = _mla_in(
                x, cos_t, sin_t, cos_tt, sin_tt, gain, w_in.astype(BF16),
                mla_q_a_norm[j].reshape(1, q_lora).astype(F32),
                _pad_qk_channels(mla_w_q_up[j], heads).astype(BF16),
                mla_kv_a_norm[j].reshape(1, kv_lora).astype(F32),
                w_knt.astype(BF16), w_v.astype(BF16),
                q_gain, k_gain_t[:LANES], k_gain_t[LANES:],
                heads=heads, tm=mla_tm)
            o = lax.cond(
                bound <= MLA_SCORE_MAX,
                functools.partial(_attention, tq=512, hg=2, bounded=True),
                functools.partial(_attention, tq=512, hg=2, bounded=False),
                qh, kth, vh)
            w_out = mla_w_out[j]
        x = _ffn(x, o, w_out.astype(BF16), norm_ffn[layer].reshape(1, d).astype(F32),
                 ffn_w_up[layer].astype(BF16), ffn_conv_w[layer].astype(F32),
                 ffn_conv_b[layer].reshape(1, -1).astype(F32), ffn_w_down[layer].astype(BF16),
                 tm=512, fc=256)
    return x
```

```python
import functools
import math

import jax
import jax.numpy as jnp
from jax import lax
from jax.experimental import pallas as pl
from jax.experimental.pallas import tpu as pltpu

F32 = jnp.float32
BF16 = jnp.bfloat16

RMS_EPS = 1e-6
ROPE_THETA = 10000.0

LANES = 128
SUBLANES = 8
VMEM_LIMIT_BYTES = 56 * 1024 * 1024

HGRN_HEAD = 128
HGRN_CHUNK = 64
HGRN_SUB = 8
MLA_NOPE = 128
MLA_ROPE = 64
MLA_QK = MLA_NOPE + MLA_ROPE
MLA_V = 128
MLA_QK_PAD = 2 * LANES
FFN_CONV = 3
ATTN_ROW_GROUP = 64
NEG_BIG = -1e30
MLA_Q_SCALE = (MLA_QK ** -0.5) * math.log2(math.e)
MLA_SCORE_MAX = 60.0

_NT = (((1,), (1,)), ((), ()))


def _compiler_params(semantics):
    return pltpu.CompilerParams(dimension_semantics=semantics, vmem_limit_bytes=VMEM_LIMIT_BYTES)


def _resident(shape, index_map):
    return pl.BlockSpec(shape, index_map, pipeline_mode=pl.Buffered(1))


def _rms_scale(x, n):
    return lax.rsqrt(jnp.sum(x * x, axis=-1, keepdims=True) * (1.0 / n) + RMS_EPS)


def _sigmoid(x):
    return 1.0 / (1.0 + jnp.exp(-x))


def _hgrn_in_kernel(x_ref, g_ref, w_ref, lb_ref, q_ref, lf_ref, v_ref, gate_ref):
    x = x_ref[...]
    d = x.shape[-1]
    h = (x * _rms_scale(x, d) * g_ref[...]).astype(BF16)
    p = jnp.dot(h, w_ref[...], preferred_element_type=F32)
    q = p[:, 0 * d:1 * d]
    f = p[:, 1 * d:2 * d]
    lb = lb_ref[...]
    forget = lb + (1.0 - lb) * _sigmoid(f)
    q_ref[...] = q * _sigmoid(q)
    lf_ref[...] = jnp.log2(forget)
    v_ref[...] = p[:, 2 * d:3 * d].astype(BF16)
    g = p[:, 3 * d:4 * d]
    gate_ref[...] = g * _sigmoid(g)


def _hgrn_in(x2d, gain, w_in, lb, tm):
    m, d = x2d.shape
    row = pl.BlockSpec((tm, d), lambda i: (i, 0))
    vec = _resident((1, d), lambda i: (0, 0))
    return pl.pallas_call(
        _hgrn_in_kernel,
        grid=(m // tm,),
        in_specs=[row, vec, _resident((d, 4 * d), lambda i: (0, 0)), vec],
        out_specs=[row, row, row, row],
        out_shape=[
            jax.ShapeDtypeStruct((m, d), F32),
            jax.ShapeDtypeStruct((m, d), F32),
            jax.ShapeDtypeStruct((m, d), BF16),
            jax.ShapeDtypeStruct((m, d), F32),
        ],
        compiler_params=_compiler_params(("arbitrary",)),
        name="hgrn_in",
    )(x2d, gain, w_in, lb)


def _cumsum_rows(x):
    c = x.shape[0]
    row = lax.broadcasted_iota(jnp.int32, (c, c), 0)
    col = lax.broadcasted_iota(jnp.int32, (c, c), 1)
    tri = jnp.where(col <= row, 1.0, 0.0).astype(BF16)
    hi = x.astype(BF16)
    rem = x - hi.astype(F32)
    mid = rem.astype(BF16)
    lo = (rem - mid.astype(F32)).astype(BF16)
    out = jnp.dot(tri, hi, preferred_element_type=F32)
    out = out + jnp.dot(tri, mid, preferred_element_type=F32)
    return out + jnp.dot(tri, lo, preferred_element_type=F32)


def _hgrn_offdiag(q, k, g_cum):
    c, sub, half = HGRN_CHUNK, HGRN_SUB, SUBLANES
    rows = [jnp.zeros((half, c), F32) for _ in range(sub // half)]
    for j in range(1, c // sub):
        r0 = j * sub
        g_ref = g_cum[r0 - 1:r0, :]
        q_off = (q[r0:r0 + sub] * jnp.exp2(g_cum[r0:r0 + sub] - g_ref)).astype(BF16)
        k_off = k[:r0] * jnp.exp2(jnp.minimum(g_ref - g_cum[:r0], 0.0))
        k_off = jnp.concatenate([k_off, jnp.zeros((c - r0, HGRN_HEAD), F32)], axis=0).astype(BF16)
        a_off = lax.dot_general(q_off, k_off, _NT, preferred_element_type=F32)
        rows += [a_off[g * half:(g + 1) * half] for g in range(sub // half)]
    return rows


def _hgrn_diag(q, k, g_cum, a_rows):
    c, sub, half = HGRN_CHUNK, HGRN_SUB, SUBLANES
    lane = lax.broadcasted_iota(jnp.int32, (half, c), 1)
    sub_row = lax.broadcasted_iota(jnp.int32, (half, HGRN_HEAD), 0)
    groups = sub // half
    a_rows = list(a_rows)
    for j in range(c // sub):
        r0 = j * sub
        for s in range(sub):
            gs = g_cum[r0 + s:r0 + s + 1]
            ks = k[r0 + s:r0 + s + 1]
            for g in range(s // half, groups):
                rows = slice(r0 + g * half, r0 + (g + 1) * half)
                diff = g_cum[rows] - gs
                if g == s // half:
                    diff = jnp.where(sub_row >= s - g * half, diff, NEG_BIG)
                col = jnp.sum(q[rows] * (ks * jnp.exp2(diff)), axis=-1, keepdims=True)
                a_rows[j * groups + g] = jnp.where(lane == r0 + s, col, a_rows[j * groups + g])
    return jnp.concatenate(a_rows, axis=0).astype(BF16)


def _hgrn_rec_kernel(q_ref, lf_ref, v_ref, gate_ref, gain_ref, o_ref, state_ref):
    @pl.when(pl.program_id(2) == 0)
    def _():
        state_ref[...] = jnp.zeros_like(state_ref)

    c = HGRN_CHUNK
    gain = gain_ref[...]
    n_heads = q_ref.shape[2] // HGRN_HEAD
    n_chunks = q_ref.shape[1] // c
    tiles = [(slice(ci * c, (ci + 1) * c), slice(hd * HGRN_HEAD, (hd + 1) * HGRN_HEAD))
             for hd in range(n_heads) for ci in range(n_chunks)]
    n = len(tiles)

    q = [q_ref[0, rows, cols] for rows, cols in tiles]
    lf = [lf_ref[0, rows, cols] for rows, cols in tiles]
    v = [v_ref[0, rows, cols] for rows, cols in tiles]
    g_cum = [_cumsum_rows(x) for x in lf]
    k = [1.0 - jnp.exp2(x) for x in lf]
    a_rows = [_hgrn_offdiag(q[i], k[i], g_cum[i]) for i in range(n)]
    g_last = [g[c - 1:c, :] for g in g_cum]
    k_dec = [(k[i] * jnp.exp2(g_last[i] - g_cum[i])).astype(BF16) for i in range(n)]
    update = [jnp.dot(v[i].astype(F32).T.astype(BF16), k_dec[i], preferred_element_type=F32) for i in range(n)]
    q_dec = [(q[i] * jnp.exp2(g_cum[i])).astype(BF16) for i in range(n)]
    o_intra = [jnp.dot(_hgrn_diag(q[i], k[i], g_cum[i], a_rows[i]), v[i], preferred_element_type=F32)
               for i in range(n)]

    for hd in range(n_heads):
        state_t = state_ref[hd]
        for ci in range(n_chunks):
            i = hd * n_chunks + ci
            rows, cols = tiles[i]
            o = o_intra[i] + lax.dot_general(q_dec[i], state_t.astype(BF16), _NT, preferred_element_type=F32)
            state_t = state_t * jnp.exp2(g_last[i]) + update[i]
            y = o * _rms_scale(o, HGRN_HEAD) * gain
            o_ref[0, rows, cols] = (y * gate_ref[0, rows, cols]).astype(BF16)
        state_ref[hd] = state_t


def _hgrn_rec(q, lf, v, gate, gain, tt, hg):
    b, s, d = q.shape
    heads = d // HGRN_HEAD
    blk = pl.BlockSpec((1, tt, hg * HGRN_HEAD), lambda bi, hi, ti: (bi, ti, hi))
    return pl.pallas_call(
        _hgrn_rec_kernel,
        grid=(b, heads // hg, s // tt),
        in_specs=[blk, blk, blk, blk, pl.BlockSpec((1, HGRN_HEAD), lambda bi, hi, ti: (0, 0))],
        out_specs=blk,
        out_shape=jax.ShapeDtypeStruct((b, s, d), BF16),
        scratch_shapes=[pltpu.VMEM((hg, HGRN_HEAD, HGRN_HEAD), F32)],
        compiler_params=_compiler_params(("arbitrary", "arbitrary", "arbitrary")),
        name="hgrn_rec",
    )(q, lf, v, gate, gain)


def _rope(t, cos_t, sin_t):
    return t * cos_t + pltpu.roll(t, LANES // 2, axis=1) * sin_t


def _mla_in_kernel(x_ref, cos_ref, sin_ref, cost_ref, sint_ref, g_ref, w_in_ref, qa_ref, w_q_ref, kva_ref,
                   w_knt_ref, w_v_ref, qn_ref, kn_nope_ref, kn_rope_ref, q_out, kt_out, v_out,
                   *, q_lora, kv_lora):
    x = x_ref[0]
    tm, d = x.shape
    heads = q_out.shape[1]
    quarter = LANES // 4
    h = (x * _rms_scale(x, d) * g_ref[...]).astype(BF16)
    c = jnp.dot(h, w_in_ref[...], preferred_element_type=F32)
    c_q = c[:, :q_lora]
    c_kv = c[:, q_lora:q_lora + kv_lora]
    k_rope = c[:, q_lora + kv_lora:]
    c_q = (c_q * _rms_scale(c_q, q_lora) * qa_ref[...]).astype(BF16)
    c_kv = (c_kv * _rms_scale(c_kv, kv_lora) * kva_ref[...]).astype(BF16)
    q = jnp.dot(c_q, w_q_ref[...], preferred_element_type=F32)
    v_all = jnp.dot(c_kv, w_v_ref[...], preferred_element_type=F32)
    kn_t = lax.dot_general(w_knt_ref[...], c_kv, _NT, preferred_element_type=F32)

    q_tiles = [q[:, hd * MLA_QK_PAD:(hd + 1) * MLA_QK_PAD] for hd in range(heads)]
    q_sq = jnp.concatenate([(t * t).astype(BF16) for t in q_tiles], axis=0)
    ss_q = jnp.dot(q_sq, jnp.ones((MLA_QK_PAD, LANES), BF16), preferred_element_type=F32)
    r_q = lax.rsqrt(ss_q * (1.0 / MLA_QK) + RMS_EPS)
    cos_t = cos_ref[0]
    sin_t = sin_ref[0]
    qn = qn_ref[...]
    for hd in range(heads):
        r = r_q[hd * tm:(hd + 1) * tm]
        t = q_tiles[hd]
        q_out[0, hd, :, :LANES] = (t[:, :LANES] * r * qn[:, :LANES]).astype(BF16)
        q_out[0, hd, :, LANES:] = _rope(t[:, LANES:] * r * qn[:, LANES:], cos_t, sin_t).astype(BF16)

    kr_t = k_rope.T
    ss_r = jnp.sum(kr_t * kr_t, axis=0, keepdims=True)
    kr_g = kr_t * kn_rope_ref[...]
    x1 = kr_g[0:quarter]
    x2 = kr_g[2 * quarter:3 * quarter]
    cos_tt = cost_ref[0]
    sin_tt = sint_ref[0]
    zero = jnp.zeros((quarter, tm), F32)
    kr_rot = jnp.concatenate([x1 * cos_tt - x2 * sin_tt, zero, x2 * cos_tt + x1 * sin_tt, zero], axis=0)
    kn_gain = kn_nope_ref[...]
    ones = jnp.ones((tm, LANES), BF16)
    for hd in range(heads):
        kn = kn_t[hd * LANES:(hd + 1) * LANES]
        ss_k = jnp.sum(kn * kn, axis=0, keepdims=True) + ss_r
        r_k = lax.rsqrt(ss_k * (1.0 / MLA_QK) + RMS_EPS)
        kt_out[0, hd, :LANES, :] = (kn * r_k * kn_gain).astype(BF16)
        kt_out[0, hd, LANES:, :] = (kr_rot * r_k).astype(BF16)
        v_out[0, hd, :, :MLA_V] = v_all[:, hd * MLA_V:(hd + 1) * MLA_V].astype(BF16)
        v_out[0, hd, :, MLA_V:] = ones


def _mla_in(x, cos_t, sin_t, cos_tt, sin_tt, gain, w_in, qa, w_q, kva, w_knt, w_v, qn, kn_nope, kn_rope,
            heads, tm):
    b, s, d = x.shape
    q_lora = qa.shape[1]
    kv_lora = kva.shape[1]
    const = lambda bi, si: (0, 0)
    row = lambda w: pl.BlockSpec((1, tm, w), lambda bi, si: (bi, si, 0))
    col = pl.BlockSpec((1, cos_tt.shape[1], tm), lambda bi, si: (bi, 0, si))
    head_out = lambda w: pl.BlockSpec((1, heads, tm, w), lambda bi, si: (bi, 0, si, 0))
    return pl.pallas_call(
        functools.partial(_mla_in_kernel, q_lora=q_lora, kv_lora=kv_lora),
        grid=(b, s // tm),
        in_specs=[
            row(d), row(LANES), row(LANES), col, col,
            _resident((1, d), const),
            _resident(w_in.shape, const),
            _resident(qa.shape, const),
            _resident(w_q.shape, const),
            _resident(kva.shape, const),
            _resident(w_knt.shape, const),
            _resident(w_v.shape, const),
            _resident(qn.shape, const),
            _resident(kn_nope.shape, const),
            _resident(kn_rope.shape, const),
        ],
        out_specs=[
            head_out(MLA_QK_PAD),
            pl.BlockSpec((1, heads, MLA_QK_PAD, tm), lambda bi, si: (bi, 0, 0, si)),
            head_out(MLA_V + LANES),
        ],
        out_shape=[
            jax.ShapeDtypeStruct((b, heads, s, MLA_QK_PAD), BF16),
            jax.ShapeDtypeStruct((b, heads, MLA_QK_PAD, s), BF16),
            jax.ShapeDtypeStruct((b, heads, s, MLA_V + LANES), BF16),
        ],
        compiler_params=_compiler_params(("arbitrary", "arbitrary")),
        name="mla_in",
    )(x, cos_t, sin_t, cos_tt, sin_tt, gain, w_in, qa, w_q, kva, w_knt, w_v, qn, kn_nope, kn_rope)


def _attn_online_kernel(q_ref, kt_ref, v_ref, o_ref, acc_ref, m_ref, alpha_ref, s0_ref, s1_ref, p_ref,
                        *, tq):
    qi = pl.program_id(2)
    hg = q_ref.shape[1]
    dv = v_ref.shape[-1] - LANES
    lane_tiles = tq // LANES

    def scores(ki, s_ref):
        cols = pl.ds(pl.multiple_of(ki * tq, tq), tq)
        for hd in range(hg):
            s_ref[hd] = jnp.dot(q_ref[0, hd], kt_ref[0, hd, :, cols], preferred_element_type=F32)

    def consume(ki, s_ref, masked):
        rows = pl.ds(pl.multiple_of(ki * tq, tq), tq)
        for hd in range(hg):
            for g in range(tq // ATTN_ROW_GROUP):
                rg = slice(g * ATTN_ROW_GROUP, (g + 1) * ATTN_ROW_GROUP)
                s = s_ref[hd, rg, :]
                if masked:
                    r = lax.broadcasted_iota(jnp.int32, s.shape, 0) + g * ATTN_ROW_GROUP
                    c = lax.broadcasted_iota(jnp.int32, s.shape, 1)
                    s = jnp.where(c <= r, s, NEG_BIG)
                m_prev = m_ref[hd, rg, :]
                m_new = jnp.maximum(m_prev, jnp.max(s, axis=-1, keepdims=True))
                alpha = jnp.exp2(m_prev - m_new)
                m_ref[hd, rg, :] = m_new
                alpha_ref[hd, rg, :LANES] = alpha
                alpha_ref[hd, rg, LANES:] = alpha
                m_wide = jnp.concatenate([m_new] * lane_tiles, axis=1)
                p_ref[hd, rg, :] = jnp.exp2(s - m_wide).astype(BF16)
            pv = jnp.dot(p_ref[hd], v_ref[0, hd, rows, :], preferred_element_type=F32)
            acc_ref[hd] = alpha_ref[hd] * acc_ref[hd] + pv

    def pair(kp, carry):
        k0 = 2 * kp
        scores(k0 + 1, s1_ref)
        consume(k0, s0_ref, False)
        scores(k0 + 2, s0_ref)
        consume(k0 + 1, s1_ref, False)
        return carry

    acc_ref[...] = jnp.zeros_like(acc_ref)
    m_ref[...] = jnp.full(m_ref.shape, NEG_BIG, F32)
    scores(0, s0_ref)
    lax.fori_loop(0, qi // 2, pair, 0)

    @pl.when(qi % 2 == 0)
    def _():
        consume(qi, s0_ref, True)

    @pl.when(qi % 2 == 1)
    def _():
        scores(qi, s1_ref)
        consume(qi - 1, s0_ref, False)
        consume(qi, s1_ref, True)

    for hd in range(hg):
        o_ref[0, :, hd * dv:(hd + 1) * dv] = (acc_ref[hd, :, :dv] / acc_ref[hd, :, dv:]).astype(o_ref.dtype)


def _attn_bounded_kernel(q_ref, kt_ref, v_ref, o_ref, acc_ref, p0_ref, p1_ref, pd_ref, *, tq):
    qi = pl.program_id(2)
    hg = q_ref.shape[1]
    dv = v_ref.shape[-1] - LANES

    def probs(ki, p_ref, masked):
        cols = pl.ds(pl.multiple_of(ki * tq, tq), tq)
        for hd in range(hg):
            s = jnp.dot(q_ref[0, hd], kt_ref[0, hd, :, cols], preferred_element_type=F32)
            if masked:
                r = lax.broadcasted_iota(jnp.int32, s.shape, 0)
                c = lax.broadcasted_iota(jnp.int32, s.shape, 1)
                s = jnp.where(c <= r, s, NEG_BIG)
            p_ref[hd] = jnp.exp2(s).astype(BF16)

    def accumulate(ki, p_ref):
        rows = pl.ds(pl.multiple_of(ki * tq, tq), tq)
        for hd in range(hg):
            acc_ref[hd] += jnp.dot(p_ref[hd], v_ref[0, hd, rows, :], preferred_element_type=F32)

    def pair(kp, carry):
        k0 = 2 * kp
        probs(k0 + 1, p1_ref, False)
        accumulate(k0, p0_ref)
        probs(k0 + 2, p0_ref, False)
        accumulate(k0 + 1, p1_ref)
        return carry

    acc_ref[...] = jnp.zeros_like(acc_ref)

    @pl.when(qi == 0)
    def _():
        probs(0, pd_ref, True)
        accumulate(0, pd_ref)

    @pl.when(qi > 0)
    def _():
        probs(0, p0_ref, False)
        lax.fori_loop(0, (qi - 1) // 2, pair, 0)

        @pl.when(qi % 2 == 1)
        def _():
            probs(qi, pd_ref, True)
            accumulate(qi - 1, p0_ref)
            accumulate(qi, pd_ref)

        @pl.when(qi % 2 == 0)
        def _():
            probs(qi - 1, p1_ref, False)
            accumulate(qi - 2, p0_ref)
            probs(qi, pd_ref, True)
            accumulate(qi - 1, p1_ref)
            accumulate(qi, pd_ref)

    for hd in range(hg):
        o_ref[0, :, hd * dv:(hd + 1) * dv] = (acc_ref[hd, :, :dv] / acc_ref[hd, :, dv:]).astype(o_ref.dtype)


def _attention(q, kt, v, tq, hg, bounded):
    b, heads, s, dq = q.shape
    dve = v.shape[-1]
    dv = dve - LANES
    acc = pltpu.VMEM((hg, tq, dve), F32)
    probs = pltpu.VMEM((hg, tq, tq), BF16)
    scores = pltpu.VMEM((hg, tq, tq), F32)
    if bounded:
        body, scratch = _attn_bounded_kernel, [acc, probs, probs, probs]
    else:
        running_max = pltpu.VMEM((hg, tq, LANES), F32)
        rescale = pltpu.VMEM((hg, tq, dve), F32)
        body, scratch = _attn_online_kernel, [acc, running_max, rescale, scores, scores, probs]
    return pl.pallas_call(
        functools.partial(body, tq=tq),
        grid=(b, heads // hg, s // tq),
        in_specs=[
            pl.BlockSpec((1, hg, tq, dq), lambda bi, hi, qi: (bi, hi, qi, 0)),
            pl.BlockSpec((1, hg, dq, s), lambda bi, hi, qi: (bi, hi, 0, 0)),
            pl.BlockSpec((1, hg, s, dve), lambda bi, hi, qi: (bi, hi, 0, 0)),
        ],
        out_specs=pl.BlockSpec((1, tq, hg * dv), lambda bi, hi, qi: (bi, qi, hi)),
        out_shape=jax.ShapeDtypeStruct((b, s, heads * dv), BF16),
        scratch_shapes=scratch,
        compiler_params=_compiler_params(("arbitrary", "arbitrary", "arbitrary")),
        name="mla_attention_bounded" if bounded else "mla_attention_online",
    )(q, kt, v)


def _ffn_kernel(x_ref, a_ref, w_o_ref, g_ref, w_up_ref, cw_ref, cb_ref, w_down_ref, o_ref, tail_ref, stage_ref,
                *, d_ff, fc):
    @pl.when(pl.program_id(1) == 0)
    def _():
        tail_ref[...] = jnp.zeros_like(tail_ref)

    x1 = x_ref[0] + jnp.dot(a_ref[0], w_o_ref[...], preferred_element_type=F32)
    tm, d = x1.shape
    h = (x1 * _rms_scale(x1, d) * g_ref[...]).astype(BF16)
    n_chunks = d_ff // fc

    for p in range(2 * n_chunks):
        cols = slice(p * fc, (p + 1) * fc)
        u = jnp.dot(h, w_up_ref[:, cols], preferred_element_type=F32)
        stage_ref[p, 0:SUBLANES, :] = tail_ref[:, cols]
        stage_ref[p, SUBLANES:SUBLANES + tm, :] = u
        tail_ref[:, cols] = u[tm - SUBLANES:tm, :]

    def conv(p):
        cols = slice(p * fc, (p + 1) * fc)
        w = cw_ref[:, cols]
        u = stage_ref[p, SUBLANES:SUBLANES + tm, :]
        u1 = stage_ref[p, SUBLANES - 1:SUBLANES - 1 + tm, :]
        u2 = stage_ref[p, SUBLANES - 2:SUBLANES - 2 + tm, :]
        return cb_ref[:, cols] + u2 * w[0:1] + u1 * w[1:2] + u * w[2:3]

    acts = []
    for j in range(n_chunks):
        gate = conv(j)
        acts.append((gate * _sigmoid(gate) * conv(n_chunks + j)).astype(BF16))

    acc = x1
    for j in range(n_chunks):
        acc = acc + jnp.dot(acts[j], w_down_ref[j * fc:(j + 1) * fc, :], preferred_element_type=F32)
    o_ref[0] = acc


def _ffn(x, a, w_o, gain, w_up, conv_w, conv_b, w_down, tm, fc):
    b, s, d = x.shape
    d_ff = w_down.shape[0]
    const = lambda bi, si: (0, 0)
    row = pl.BlockSpec((1, tm, d), lambda bi, si: (bi, si, 0))
    return pl.pallas_call(
        functools.partial(_ffn_kernel, d_ff=d_ff, fc=fc),
        grid=(b, s // tm),
        in_specs=[
            row,
            pl.BlockSpec((1, tm, a.shape[-1]), lambda bi, si: (bi, si, 0)),
            _resident(w_o.shape, const),
            _resident((1, d), const),
            _resident(w_up.shape, const),
            _resident(conv_w.shape, const),
            _resident(conv_b.shape, const),
            _resident(w_down.shape, const),
        ],
        out_specs=row,
        out_shape=jax.ShapeDtypeStruct((b, s, d), F32),
        scratch_shapes=[
            pltpu.VMEM((SUBLANES, 2 * d_ff), F32),
            pltpu.VMEM((2 * d_ff // fc, SUBLANES + tm, fc), F32),
        ],
        compiler_params=_compiler_params(("arbitrary", "arbitrary")),
        name="conv_ffn",
    )(x, a, w_o, gain, w_up, conv_w, conv_b, w_down)


def _spread_rope(w):
    half = MLA_ROPE // 2
    z = jnp.zeros(w.shape[:-1] + (LANES // 2 - half,), w.dtype)
    return jnp.concatenate([w[..., :half], z, w[..., half:], z], axis=-1)


def _pad_qk_channels(w, heads):
    w = w.reshape(w.shape[:-1] + (heads, MLA_QK))
    w = jnp.concatenate([w[..., :MLA_NOPE], _spread_rope(w[..., MLA_NOPE:])], axis=-1)
    return w.reshape(w.shape[:-2] + (heads * MLA_QK_PAD,))


def kernel(x, positions, norm_mix, norm_ffn, hgrn_w_in, hgrn_lower_bounds, hgrn_out_norm, hgrn_w_out,
           mla_w_in, mla_q_a_norm, mla_w_q_up, mla_kv_a_norm, mla_w_kv_up, mla_q_norm, mla_k_norm,
           mla_w_out, ffn_w_up, ffn_conv_w, ffn_conv_b, ffn_w_down):
    b, s, d = x.shape
    depth = norm_mix.shape[0]
    n_mixers = 2
    q_lora = mla_q_a_norm.shape[1]
    kv_lora = mla_kv_a_norm.shape[1]
    heads = mla_w_q_up.shape[2] // MLA_QK
    mla_tm = 512

    lb_soft = jax.nn.softmax(hgrn_lower_bounds.astype(F32), axis=0)
    lower_bounds = jnp.cumsum(lb_soft, axis=0) - lb_soft[0:1]
    inv_freq = ROPE_THETA ** (-jnp.arange(0, MLA_ROPE, 2, dtype=F32) / MLA_ROPE)
    ang = positions.astype(F32)[..., None] * inv_freq
    cos, sin = jnp.cos(ang), jnp.sin(ang)
    zeros = jnp.zeros_like(cos)
    cos_t = jnp.concatenate([cos, zeros, cos, zeros], axis=-1)
    sin_t = jnp.concatenate([-sin, zeros, sin, zeros], axis=-1)
    cos_tt = jnp.swapaxes(cos, 1, 2)
    sin_tt = jnp.swapaxes(sin, 1, 2)

    x = x.astype(F32)
    for layer in range(depth):
        j = layer // n_mixers
        gain = norm_mix[layer].reshape(1, d).astype(F32)
        if layer % n_mixers == 0:
            q, lf, v, gate = _hgrn_in(
                x.reshape(b * s, d), gain, hgrn_w_in[j].astype(BF16),
                lower_bounds[j].reshape(1, d), tm=512)
            shp = (b, s, d)
            o = _hgrn_rec(q.reshape(shp), lf.reshape(shp), v.reshape(shp), gate.reshape(shp),
                          hgrn_out_norm[j].reshape(1, HGRN_HEAD).astype(F32), tt=512, hg=2)
            w_out = hgrn_w_out[j]
        else:
            w_in = jnp.concatenate(
                [mla_w_in[j][:, :q_lora + kv_lora], _spread_rope(mla_w_in[j][:, q_lora + kv_lora:])], axis=-1)
            w_kv = mla_w_kv_up[j].reshape(kv_lora, heads, MLA_NOPE + MLA_V)
            w_knt = jnp.transpose(w_kv[..., :MLA_NOPE], (1, 2, 0)).reshape(heads * MLA_NOPE, kv_lora)
            w_v = w_kv[..., MLA_NOPE:].reshape(kv_lora, heads * MLA_V)
            q_gain = _pad_qk_channels(mla_q_norm[j].reshape(1, MLA_QK), 1).astype(F32) * MLA_Q_SCALE
            k_gain = _pad_qk_channels(mla_k_norm[j].reshape(1, MLA_QK), 1).astype(F32)
            k_gain_t = jnp.broadcast_to(k_gain.reshape(MLA_QK_PAD, 1), (MLA_QK_PAD, mla_tm))
            bound = 1.02 * MLA_QK * MLA_Q_SCALE * jnp.max(jnp.abs(mla_q_norm[j])) * jnp.max(jnp.abs(mla_k_norm[j]))
            qh, kth, vh = _mla_in(
                x, cos_t, sin_t, cos_tt, sin_tt, gain, w_in.astype(BF16),
                mla_q_a_norm[j].reshape(1, q_lora).astype(F32),
                _pad_qk_channels(mla_w_q_up[j], heads).astype(BF16),
                mla_kv_a_norm[j].reshape(1, kv_lora).astype(F32),
                w_knt.astype(BF16), w_v.astype(BF16),
                q_gain, k_gain_t[:LANES], k_gain_t[LANES:],
                heads=heads, tm=mla_tm)
            o = lax.cond(
                bound <= MLA_SCORE_MAX,
                functools.partial(_attention, tq=512, hg=2, bounded=True),
                functools.partial(_attention, tq=512, hg=2, bounded=False),
                qh, kth, vh)
            w_out = mla_w_out[j]
        x = _ffn(x, o, w_out.astype(BF16), norm_ffn[layer].reshape(1, d).astype(F32),
                 ffn_w_up[layer].astype(BF16), ffn_conv_w[layer].astype(F32),
                 ffn_conv_b[layer].reshape(1, -1).astype(F32), ffn_w_down[layer].astype(BF16),
                 tm=512, fc=256)
    return x
```

```python
import functools
import math

import jax
import jax.numpy as jnp
from jax import lax
from jax.experimental import pallas as pl
from jax.experimental.pallas import tpu as pltpu

F32 = jnp.float32
BF16 = jnp.bfloat16

RMS_EPS = 1e-6
ROPE_THETA = 10000.0

LANES = 128
SUBLANES = 8
VMEM_LIMIT_BYTES = 56 * 1024 * 1024

HGRN_HEAD = 128
HGRN_CHUNK = 64
HGRN_SUB = 8
MLA_NOPE = 128
MLA_ROPE = 64
MLA_QK = MLA_NOPE + MLA_ROPE
MLA_V = 128
MLA_QK_PAD = 2 * LANES
FFN_CONV = 3
ATTN_ROW_GROUP = 64
ATTN_UNROLL = 4
NEG_BIG = -1e30
MLA_Q_SCALE = (MLA_QK ** -0.5) * math.log2(math.e)
MLA_SCORE_MAX = 60.0

_NT = (((1,), (1,)), ((), ()))


def _compiler_params(semantics):
    return pltpu.CompilerParams(dimension_semantics=semantics, vmem_limit_bytes=VMEM_LIMIT_BYTES)


def _resident(shape, index_map):
    return pl.BlockSpec(shape, index_map, pipeline_mode=pl.Buffered(1))


def _rms_scale(x, n):
    return lax.rsqrt(jnp.sum(x * x, axis=-1, keepdims=True) * (1.0 / n) + RMS_EPS)


def _sigmoid(x):
    return 1.0 / (1.0 + jnp.exp(-x))


def _hgrn_in_kernel(x_ref, g_ref, w_ref, lb_ref, q_ref, lf_ref, v_ref, gate_ref):
    x = x_ref[...]
    d = x.shape[-1]
    h = (x * _rms_scale(x, d) * g_ref[...]).astype(BF16)
    p = jnp.dot(h, w_ref[...], preferred_element_type=F32)
    q = p[:, 0 * d:1 * d]
    f = p[:, 1 * d:2 * d]
    lb = lb_ref[...]
    forget = lb + (1.0 - lb) * _sigmoid(f)
    q_ref[...] = q * _sigmoid(q)
    lf_ref[...] = jnp.log2(forget)
    v_ref[...] = p[:, 2 * d:3 * d].astype(BF16)
    g = p[:, 3 * d:4 * d]
    gate_ref[...] = g * _sigmoid(g)


def _hgrn_in(x2d, gain, w_in, lb, tm):
    m, d = x2d.shape
    row = pl.BlockSpec((tm, d), lambda i: (i, 0))
    vec = _resident((1, d), lambda i: (0, 0))
    return pl.pallas_call(
        _hgrn_in_kernel,
        grid=(m // tm,),
        in_specs=[row, vec, _resident((d, 4 * d), lambda i: (0, 0)), vec],
        out_specs=[row, row, row, row],
        out_shape=[
            jax.ShapeDtypeStruct((m, d), F32),
            jax.ShapeDtypeStruct((m, d), F32),
            jax.ShapeDtypeStruct((m, d), BF16),
            jax.ShapeDtypeStruct((m, d), F32),
        ],
        compiler_params=_compiler_params(("arbitrary",)),
        name="hgrn_in",
    )(x2d, gain, w_in, lb)


def _cumsum_rows(x):
    c = x.shape[0]
    row = lax.broadcasted_iota(jnp.int32, (c, c), 0)
    col = lax.broadcasted_iota(jnp.int32, (c, c), 1)
    tri = jnp.where(col <= row, 1.0, 0.0).astype(BF16)
    hi = x.astype(BF16)
    rem = x - hi.astype(F32)
    mid = rem.astype(BF16)
    lo = (rem - mid.astype(F32)).astype(BF16)
    out = jnp.dot(tri, hi, preferred_element_type=F32)
    out = out + jnp.dot(tri, mid, preferred_element_type=F32)
    return out + jnp.dot(tri, lo, preferred_element_type=F32)


def _hgrn_offdiag(q, k, g_cum):
    c, sub, half = HGRN_CHUNK, HGRN_SUB, SUBLANES
    rows = [jnp.zeros((half, c), F32) for _ in range(sub // half)]
    for j in range(1, c // sub):
        r0 = j * sub
        g_ref = g_cum[r0 - 1:r0, :]
        q_off = (q[r0:r0 + sub] * jnp.exp2(g_cum[r0:r0 + sub] - g_ref)).astype(BF16)
        k_off = k[:r0] * jnp.exp2(jnp.minimum(g_ref - g_cum[:r0], 0.0))
        k_off = jnp.concatenate([k_off, jnp.zeros((c - r0, HGRN_HEAD), F32)], axis=0).astype(BF16)
        a_off = lax.dot_general(q_off, k_off, _NT, preferred_element_type=F32)
        rows += [a_off[g * half:(g + 1) * half] for g in range(sub // half)]
    return rows


def _hgrn_diag(q, k, g_cum, a_rows):
    c, sub, half = HGRN_CHUNK, HGRN_SUB, SUBLANES
    lane = lax.broadcasted_iota(jnp.int32, (half, c), 1)
    sub_row = lax.broadcasted_iota(jnp.int32, (half, HGRN_HEAD), 0)
    groups = sub // half
    a_rows = list(a_rows)
    for j in range(c // sub):
        r0 = j * sub
        for s in range(sub):
            gs = g_cum[r0 + s:r0 + s + 1]
            ks = k[r0 + s:r0 + s + 1]
            for g in range(s // half, groups):
                rows = slice(r0 + g * half, r0 + (g + 1) * half)
                diff = g_cum[rows] - gs
                if g == s // half:
                    diff = jnp.where(sub_row >= s - g * half, diff, NEG_BIG)
                col = jnp.sum(q[rows] * (ks * jnp.exp2(diff)), axis=-1, keepdims=True)
                a_rows[j * groups + g] = jnp.where(lane == r0 + s, col, a_rows[j * groups + g])
    return jnp.concatenate(a_rows, axis=0).astype(BF16)


def _hgrn_rec_kernel(q_ref, lf_ref, v_ref, gate_ref, gain_ref, o_ref, state_ref):
    @pl.when(pl.program_id(2) == 0)
    def _():
        state_ref[...] = jnp.zeros_like(state_ref)

    c = HGRN_CHUNK
    gain = gain_ref[...]
    n_heads = q_ref.shape[2] // HGRN_HEAD
    n_chunks = q_ref.shape[1] // c
    tiles = [(slice(ci * c, (ci + 1) * c), slice(hd * HGRN_HEAD, (hd + 1) * HGRN_HEAD))
             for hd in range(n_heads) for ci in range(n_chunks)]
    n = len(tiles)

    q = [q_ref[0, rows, cols] for rows, cols in tiles]
    lf = [lf_ref[0, rows, cols] for rows, cols in tiles]
    v = [v_ref[0, rows, cols] for rows, cols in tiles]
    g_cum = [_cumsum_rows(x) for x in lf]
    k = [1.0 - jnp.exp2(x) for x in lf]
    a_rows = [_hgrn_offdiag(q[i], k[i], g_cum[i]) for i in range(n)]
    g_last = [g[c - 1:c, :] for g in g_cum]
    k_dec = [(k[i] * jnp.exp2(g_last[i] - g_cum[i])).astype(BF16) for i in range(n)]
    update = [jnp.dot(v[i].astype(F32).T.astype(BF16), k_dec[i], preferred_element_type=F32) for i in range(n)]
    q_dec = [(q[i] * jnp.exp2(g_cum[i])).astype(BF16) for i in range(n)]
    o_intra = [jnp.dot(_hgrn_diag(q[i], k[i], g_cum[i], a_rows[i]), v[i], preferred_element_type=F32)
               for i in range(n)]

    for hd in range(n_heads):
        state_t = state_ref[hd]
        for ci in range(n_chunks):
            i = hd * n_chunks + ci
            rows, cols = tiles[i]
            o = o_intra[i] + lax.dot_general(q_dec[i], state_t.astype(BF16), _NT, preferred_element_type=F32)
            state_t = state_t * jnp.exp2(g_last[i]) + update[i]
            y = o * _rms_scale(o, HGRN_HEAD) * gain
            o_ref[0, rows, cols] = (y * gate_ref[0, rows, cols]).astype(BF16)
        state_ref[hd] = state_t


def _hgrn_rec(q, lf, v, gate, gain, tt, hg):
    b, s, d = q.shape
    heads = d // HGRN_HEAD
    blk = pl.BlockSpec((1, tt, hg * HGRN_HEAD), lambda bi, hi, ti: (bi, ti, hi))
    return pl.pallas_call(
        _hgrn_rec_kernel,
        grid=(b, heads // hg, s // tt),
        in_specs=[blk, blk, blk, blk, pl.BlockSpec((1, HGRN_HEAD), lambda bi, hi, ti: (0, 0))],
        out_specs=blk,
        out_shape=jax.ShapeDtypeStruct((b, s, d), BF16),
        scratch_shapes=[pltpu.VMEM((hg, HGRN_HEAD, HGRN_HEAD), F32)],
        compiler_params=_compiler_params(("arbitrary", "arbitrary", "arbitrary")),
        name="hgrn_rec",
    )(q, lf, v, gate, gain)


def _rope(t, cos_t, sin_t):
    return t * cos_t + pltpu.roll(t, LANES // 2, axis=1) * sin_t


def _mla_in_kernel(x_ref, cos_ref, sin_ref, cost_ref, sint_ref, g_ref, w_in_ref, qa_ref, w_q_ref, kva_ref,
                   w_knt_ref, w_v_ref, qn_ref, kn_nope_ref, kn_rope_ref, q_out, kt_out, v_out,
                   *, q_lora, kv_lora):
    x = x_ref[0]
    tm, d = x.shape
    heads = q_out.shape[1]
    quarter = LANES // 4
    h = (x * _rms_scale(x, d) * g_ref[...]).astype(BF16)
    c = jnp.dot(h, w_in_ref[...], preferred_element_type=F32)
    c_q = c[:, :q_lora]
    c_kv = c[:, q_lora:q_lora + kv_lora]
    k_rope = c[:, q_lora + kv_lora:]
    c_q = (c_q * _rms_scale(c_q, q_lora) * qa_ref[...]).astype(BF16)
    c_kv = (c_kv * _rms_scale(c_kv, kv_lora) * kva_ref[...]).astype(BF16)
    q = jnp.dot(c_q, w_q_ref[...], preferred_element_type=F32)
    v_all = jnp.dot(c_kv, w_v_ref[...], preferred_element_type=F32)
    kn_t = lax.dot_general(w_knt_ref[...], c_kv, _NT, preferred_element_type=F32)

    q_tiles = [q[:, hd * MLA_QK_PAD:(hd + 1) * MLA_QK_PAD] for hd in range(heads)]
    q_sq = jnp.concatenate([(t * t).astype(BF16) for t in q_tiles], axis=0)
    ss_q = jnp.dot(q_sq, jnp.ones((MLA_QK_PAD, LANES), BF16), preferred_element_type=F32)
    r_q = lax.rsqrt(ss_q * (1.0 / MLA_QK) + RMS_EPS)
    cos_t = cos_ref[0]
    sin_t = sin_ref[0]
    qn = qn_ref[...]
    for hd in range(heads):
        r = r_q[hd * tm:(hd + 1) * tm]
        t = q_tiles[hd]
        q_out[0, hd, :, :LANES] = (t[:, :LANES] * r * qn[:, :LANES]).astype(BF16)
        q_out[0, hd, :, LANES:] = _rope(t[:, LANES:] * r * qn[:, LANES:], cos_t, sin_t).astype(BF16)

    kr_t = k_rope.T
    ss_r = jnp.sum(kr_t * kr_t, axis=0, keepdims=True)
    kr_g = kr_t * kn_rope_ref[...]
    x1 = kr_g[0:quarter]
    x2 = kr_g[2 * quarter:3 * quarter]
    cos_tt = cost_ref[0]
    sin_tt = sint_ref[0]
    zero = jnp.zeros((quarter, tm), F32)
    kr_rot = jnp.concatenate([x1 * cos_tt - x2 * sin_tt, zero, x2 * cos_tt + x1 * sin_tt, zero], axis=0)
    kn_gain = kn_nope_ref[...]
    ones = jnp.ones((tm, LANES), BF16)
    for hd in range(heads):
        kn = kn_t[hd * LANES:(hd + 1) * LANES]
        ss_k = jnp.sum(kn * kn, axis=0, keepdims=True) + ss_r
        r_k = lax.rsqrt(ss_k * (1.0 / MLA_QK) + RMS_EPS)
        kt_out[0, hd, :LANES, :] = (kn * r_k * kn_gain).astype(BF16)
        kt_out[0, hd, LANES:, :] = (kr_rot * r_k).astype(BF16)
        v_out[0, hd, :, :MLA_V] = v_all[:, hd * MLA_V:(hd + 1) * MLA_V].astype(BF16)
        v_out[0, hd, :, MLA_V:] = ones


def _mla_in(x, cos_t, sin_t, cos_tt, sin_tt, gain, w_in, qa, w_q, kva, w_knt, w_v, qn, kn_nope, kn_rope,
            heads, tm):
    b, s, d = x.shape
    q_lora = qa.shape[1]
    kv_lora = kva.shape[1]
    const = lambda bi, si: (0, 0)
    row = lambda w: pl.BlockSpec((1, tm, w), lambda bi, si: (bi, si, 0))
    col = pl.BlockSpec((1, cos_tt.shape[1], tm), lambda bi, si: (bi, 0, si))
    head_out = lambda w: pl.BlockSpec((1, heads, tm, w), lambda bi, si: (bi, 0, si, 0))
    return pl.pallas_call(
        functools.partial(_mla_in_kernel, q_lora=q_lora, kv_lora=kv_lora),
        grid=(b, s // tm),
        in_specs=[
            row(d), row(LANES), row(LANES), col, col,
            _resident((1, d), const),
            _resident(w_in.shape, const),
            _resident(qa.shape, const),
            _resident(w_q.shape, const),
            _resident(kva.shape, const),
            _resident(w_knt.shape, const),
            _resident(w_v.shape, const),
            _resident(qn.shape, const),
            _resident(kn_nope.shape, const),
            _resident(kn_rope.shape, const),
        ],
        out_specs=[
            head_out(MLA_QK_PAD),
            pl.BlockSpec((1, heads, MLA_QK_PAD, tm), lambda bi, si: (bi, 0, 0, si)),
            head_out(MLA_V + LANES),
        ],
        out_shape=[
            jax.ShapeDtypeStruct((b, heads, s, MLA_QK_PAD), BF16),
            jax.ShapeDtypeStruct((b, heads, MLA_QK_PAD, s), BF16),
            jax.ShapeDtypeStruct((b, heads, s, MLA_V + LANES), BF16),
        ],
        compiler_params=_compiler_params(("arbitrary", "arbitrary")),
        name="mla_in",
    )(x, cos_t, sin_t, cos_tt, sin_tt, gain, w_in, qa, w_q, kva, w_knt, w_v, qn, kn_nope, kn_rope)


def _attn_online_kernel(q_ref, kt_ref, v_ref, o_ref, acc_ref, m_ref, alpha_ref, s0_ref, s1_ref, p_ref,
                        *, tq):
    qi = pl.program_id(2)
    hg = q_ref.shape[1]
    dv = v_ref.shape[-1] - LANES
    lane_tiles = tq // LANES

    def scores(ki, s_ref):
        cols = pl.ds(pl.multiple_of(ki * tq, tq), tq)
        for hd in range(hg):
            s_ref[hd] = jnp.dot(q_ref[0, hd], kt_ref[0, hd, :, cols], preferred_element_type=F32)

    def consume(ki, s_ref, masked):
        rows = pl.ds(pl.multiple_of(ki * tq, tq), tq)
        for hd in range(hg):
            for g in range(tq // ATTN_ROW_GROUP):
                rg = slice(g * ATTN_ROW_GROUP, (g + 1) * ATTN_ROW_GROUP)
                s = s_ref[hd, rg, :]
                if masked:
                    r = lax.broadcasted_iota(jnp.int32, s.shape, 0) + g * ATTN_ROW_GROUP
                    c = lax.broadcasted_iota(jnp.int32, s.shape, 1)
                    s = jnp.where(c <= r, s, NEG_BIG)
                m_prev = m_ref[hd, rg, :]
                m_new = jnp.maximum(m_prev, jnp.max(s, axis=-1, keepdims=True))
                alpha = jnp.exp2(m_prev - m_new)
                m_ref[hd, rg, :] = m_new
                alpha_ref[hd, rg, :LANES] = alpha
                alpha_ref[hd, rg, LANES:] = alpha
                m_wide = jnp.concatenate([m_new] * lane_tiles, axis=1)
                p_ref[hd, rg, :] = jnp.exp2(s - m_wide).astype(BF16)
            pv = jnp.dot(p_ref[hd], v_ref[0, hd, rows, :], preferred_element_type=F32)
            acc_ref[hd] = alpha_ref[hd] * acc_ref[hd] + pv

    def pair(kp, carry):
        k0 = 2 * kp
        scores(k0 + 1, s1_ref)
        consume(k0, s0_ref, False)
        scores(k0 + 2, s0_ref)
        consume(k0 + 1, s1_ref, False)
        return carry

    acc_ref[...] = jnp.zeros_like(acc_ref)
    m_ref[...] = jnp.full(m_ref.shape, NEG_BIG, F32)
    scores(0, s0_ref)
    lax.fori_loop(0, qi // 2, pair, 0)

    @pl.when(qi % 2 == 0)
    def _():
        consume(qi, s0_ref, True)

    @pl.when(qi % 2 == 1)
    def _():
        scores(qi, s1_ref)
        consume(qi - 1, s0_ref, False)
        consume(qi, s1_ref, True)

    for hd in range(hg):
        o_ref[0, :, hd * dv:(hd + 1) * dv] = (acc_ref[hd, :, :dv] / acc_ref[hd, :, dv:]).astype(o_ref.dtype)


def _attn_bounded_kernel(q_ref, kt_ref, v_ref, o_ref, acc_ref, pd_ref, *p_refs, tq):
    qi = pl.program_id(2)
    hg = q_ref.shape[1]
    dv = v_ref.shape[-1] - LANES
    unroll = len(p_refs)

    def probs(ki, p_ref, masked):
        cols = pl.ds(pl.multiple_of(ki * tq, tq), tq)
        for hd in range(hg):
            s = jnp.dot(q_ref[0, hd], kt_ref[0, hd, :, cols], preferred_element_type=F32)
            if masked:
                r = lax.broadcasted_iota(jnp.int32, s.shape, 0)
                c = lax.broadcasted_iota(jnp.int32, s.shape, 1)
                s = jnp.where(c <= r, s, NEG_BIG)
            p_ref[hd] = jnp.exp2(s).astype(BF16)

    def accumulate(ki, p_ref):
        rows = pl.ds(pl.multiple_of(ki * tq, tq), tq)
        for hd in range(hg):
            acc_ref[hd] += jnp.dot(p_ref[hd], v_ref[0, hd, rows, :], preferred_element_type=F32)

    def steps(k0, count):
        for i in range(count):
            probs(k0 + i + 1, p_refs[(i + 1) % unroll], False)
            accumulate(k0 + i, p_refs[i])

    def group(kg, carry):
        steps(unroll * kg, unroll)
        return carry

    acc_ref[...] = jnp.zeros_like(acc_ref)

    @pl.when(qi == 0)
    def _():
        probs(0, pd_ref, True)
        accumulate(0, pd_ref)

    @pl.when(qi > 0)
    def _():
        probs(0, p_refs[0], False)
        n_steps = qi - 1
        lax.fori_loop(0, n_steps // unroll, group, 0)
        done = (n_steps // unroll) * unroll
        for rem in range(unroll):
            @pl.when(n_steps % unroll == rem)
            def _(rem=rem):
                steps(done, rem)
                probs(qi, pd_ref, True)
                accumulate(qi - 1, p_refs[rem])
                accumulate(qi, pd_ref)

    for hd in range(hg):
        o_ref[0, :, hd * dv:(hd + 1) * dv] = (acc_ref[hd, :, :dv] / acc_ref[hd, :, dv:]).astype(o_ref.dtype)


def _attention(q, kt, v, tq, hg, bounded):
    b, heads, s, dq = q.shape
    dve = v.shape[-1]
    dv = dve - LANES
    acc = pltpu.VMEM((hg, tq, dve), F32)
    probs = pltpu.VMEM((hg, tq, tq), BF16)
    scores = pltpu.VMEM((hg, tq, tq), F32)
    if bounded:
        body, scratch = _attn_bounded_kernel, [acc, probs] + [probs] * ATTN_UNROLL
    else:
        running_max = pltpu.VMEM((hg, tq, LANES), F32)
        rescale = pltpu.VMEM((hg, tq, dve), F32)
        body, scratch = _attn_online_kernel, [acc, running_max, rescale, scores, scores, probs]
    return pl.pallas_call(
        functools.partial(body, tq=tq),
        grid=(b, heads // hg, s // tq),
        in_specs=[
            pl.BlockSpec((1, hg, tq, dq), lambda bi, hi, qi: (bi, hi, qi, 0)),
            pl.BlockSpec((1, hg, dq, s), lambda bi, hi, qi: (bi, hi, 0, 0)),
            pl.BlockSpec((1, hg, s, dve), lambda bi, hi, qi: (bi, hi, 0, 0)),
        ],
        out_specs=pl.BlockSpec((1, tq, hg * dv), lambda bi, hi, qi: (bi, qi, hi)),
        out_shape=jax.ShapeDtypeStruct((b, s, heads * dv), BF16),
        scratch_shapes=scratch,
        compiler_params=_compiler_params(("arbitrary", "arbitrary", "arbitrary")),
        name="mla_attention_bounded" if bounded else "mla_attention_online",
    )(q, kt, v)


def _ffn_kernel(x_ref, a_ref, w_o_ref, g_ref, w_up_ref, cw_ref, cb_ref, w_down_ref, o_ref, tail_ref, stage_ref,
                *, d_ff, fc):
    @pl.when(pl.program_id(1) == 0)
    def _():
        tail_ref[...] = jnp.zeros_like(tail_ref)

    x1 = x_ref[0] + jnp.dot(a_ref[0], w_o_ref[...], preferred_element_type=F32)
    tm, d = x1.shape
    h = (x1 * _rms_scale(x1, d) * g_ref[...]).astype(BF16)
    n_chunks = d_ff // fc

    for p in range(2 * n_chunks):
        cols = slice(p * fc, (p + 1) * fc)
        u = jnp.dot(h, w_up_ref[:, cols], preferred_element_type=F32)
        stage_ref[p, 0:SUBLANES, :] = tail_ref[:, cols]
        stage_ref[p, SUBLANES:SUBLANES + tm, :] = u
        tail_ref[:, cols] = u[tm - SUBLANES:tm, :]

    def conv(p):
        cols = slice(p * fc, (p + 1) * fc)
        w = cw_ref[:, cols]
        u = stage_ref[p, SUBLANES:SUBLANES + tm, :]
        u1 = stage_ref[p, SUBLANES - 1:SUBLANES - 1 + tm, :]
        u2 = stage_ref[p, SUBLANES - 2:SUBLANES - 2 + tm, :]
        return cb_ref[:, cols] + u2 * w[0:1] + u1 * w[1:2] + u * w[2:3]

    acts = []
    for j in range(n_chunks):
        gate = conv(j)
        acts.append((gate * _sigmoid(gate) * conv(n_chunks + j)).astype(BF16))

    acc = x1
    for j in range(n_chunks):
        acc = acc + jnp.dot(acts[j], w_down_ref[j * fc:(j + 1) * fc, :], preferred_element_type=F32)
    o_ref[0] = acc


def _ffn(x, a, w_o, gain, w_up, conv_w, conv_b, w_down, tm, fc):
    b, s, d = x.shape
    d_ff = w_down.shape[0]
    const = lambda bi, si: (0, 0)
    row = pl.BlockSpec((1, tm, d), lambda bi, si: (bi, si, 0))
    return pl.pallas_call(
        functools.partial(_ffn_kernel, d_ff=d_ff, fc=fc),
        grid=(b, s // tm),
        in_specs=[
            row,
            pl.BlockSpec((1, tm, a.shape[-1]), lambda bi, si: (bi, si, 0)),
            _resident(w_o.shape, const),
            _resident((1, d), const),
            _resident(w_up.shape, const),
            _resident(conv_w.shape, const),
            _resident(conv_b.shape, const),
            _resident(w_down.shape, const),
        ],
        out_specs=row,
        out_shape=jax.ShapeDtypeStruct((b, s, d), F32),
        scratch_shapes=[
            pltpu.VMEM((SUBLANES, 2 * d_ff), F32),
            pltpu.VMEM((2 * d_ff // fc, SUBLANES + tm, fc), F32),
        ],
        compiler_params=_compiler_params(("arbitrary", "arbitrary")),
        name="conv_ffn",
    )(x, a, w_o, gain, w_up, conv_w, conv_b, w_down)


def _spread_rope(w):
    half = MLA_ROPE // 2
    z = jnp.zeros(w.shape[:-1] + (LANES // 2 - half,), w.dtype)
    return jnp.concatenate([w[..., :half], z, w[..., half:], z], axis=-1)


def _pad_qk_channels(w, heads):
    w = w.reshape(w.shape[:-1] + (heads, MLA_QK))
    w = jnp.concatenate([w[..., :MLA_NOPE], _spread_rope(w[..., MLA_NOPE:])], axis=-1)
    return w.reshape(w.shape[:-2] + (heads * MLA_QK_PAD,))


def kernel(x, positions, norm_mix, norm_ffn, hgrn_w_in, hgrn_lower_bounds, hgrn_out_norm, hgrn_w_out,
           mla_w_in, mla_q_a_norm, mla_w_q_up, mla_kv_a_norm, mla_w_kv_up, mla_q_norm, mla_k_norm,
           mla_w_out, ffn_w_up, ffn_conv_w, ffn_conv_b, ffn_w_down):
    b, s, d = x.shape
    depth = norm_mix.shape[0]
    n_mixers = 2
    q_lora = mla_q_a_norm.shape[1]
    kv_lora = mla_kv_a_norm.shape[1]
    heads = mla_w_q_up.shape[2] // MLA_QK
    mla_tm = 512

    lb_soft = jax.nn.softmax(hgrn_lower_bounds.astype(F32), axis=0)
    lower_bounds = jnp.cumsum(lb_soft, axis=0) - lb_soft[0:1]
    inv_freq = ROPE_THETA ** (-jnp.arange(0, MLA_ROPE, 2, dtype=F32) / MLA_ROPE)
    ang = positions.astype(F32)[..., None] * inv_freq
    cos, sin = jnp.cos(ang), jnp.sin(ang)
    zeros = jnp.zeros_like(cos)
    cos_t = jnp.concatenate([cos, zeros, cos, zeros], axis=-1)
    sin_t = jnp.concatenate([-sin, zeros, sin, zeros], axis=-1)
    cos_tt = jnp.swapaxes(cos, 1, 2)
    sin_tt = jnp.swapaxes(sin, 1, 2)

    x = x.astype(F32)
    for layer in range(depth):
        j = layer // n_mixers
        gain = norm_mix[layer].reshape(1, d).astype(F32)
        if layer % n_mixers == 0:
            q, lf, v, gate = _hgrn_in(
                x.reshape(b * s, d), gain, hgrn_w_in[j].astype(BF16),
                lower_bounds[j].reshape(1, d), tm=512)
            shp = (b, s, d)
            o = _hgrn_rec(q.reshape(shp), lf.reshape(shp), v.reshape(shp), gate.reshape(shp),
                          hgrn_out_norm[j].reshape(1, HGRN_HEAD).astype(F32), tt=512, hg=2)
            w_out = hgrn_w_out[j]
        else:
            w_in = jnp.concatenate(
                [mla_w_in[j][:, :q_lora + kv_lora], _spread_rope(mla_w_in[j][:, q_lora + kv_lora:])], axis=-1)
            w_kv = mla_w_kv_up[j].reshape(kv_lora, heads, MLA_NOPE + MLA_V)
            w_knt = jnp.transpose(w_kv[..., :MLA_NOPE], (1, 2, 0)).reshape(heads * MLA_NOPE, kv_lora)
            w_v = w_kv[..., MLA_NOPE:].reshape(kv_lora, heads * MLA_V)
            q_gain = _pad_qk_channels(mla_q_norm[j].reshape(1, MLA_QK), 1).astype(F32) * MLA_Q_SCALE
            k_gain = _pad_qk_channels(mla_k_norm[j].reshape(1, MLA_QK), 1).astype(F32)
            k_gain_t = jnp.broadcast_to(k_gain.reshape(MLA_QK_PAD, 1), (MLA_QK_PAD, mla_tm))
            bound = 1.02 * MLA_QK * MLA_Q_SCALE * jnp.max(jnp.abs(mla_q_norm[j])) * jnp.max(jnp.abs(mla_k_norm[j]))
            qh, kth, vh = _mla_in(
                x, cos_t, sin_t, cos_tt, sin_tt, gain, w_in.astype(BF16),
                mla_q_a_norm[j].reshape(1, q_lora).astype(F32),
                _pad_qk_channels(mla_w_q_up[j], heads).astype(BF16),
                mla_kv_a_norm[j].reshape(1, kv_lora).astype(F32),
                w_knt.astype(BF16), w_v.astype(BF16),
                q_gain, k_gain_t[:LANES], k_gain_t[LANES:],
                heads=heads, tm=mla_tm)
            o = lax.cond(
                bound <= MLA_SCORE_MAX,
                functools.partial(_attention, tq=512, hg=2, bounded=True),
                functools.partial(_attention, tq=512, hg=2, bounded=False),
                qh, kth, vh)
            w_out = mla_w_out[j]
        x = _ffn(x, o, w_out.astype(BF16), norm_ffn[layer].reshape(1, d).astype(F32),
                 ffn_w_up[layer].astype(BF16), ffn_conv_w[layer].astype(F32),
                 ffn_conv_b[layer].reshape(1, -1).astype(F32), ffn_w_down[layer].astype(BF16),
                 tm=512, fc=256)
    return x
```

```python
import functools
import math

import jax
import jax.numpy as jnp
from jax import lax
from jax.experimental import pallas as pl
from jax.experimental.pallas import tpu as pltpu

F32 = jnp.float32
BF16 = jnp.bfloat16

RMS_EPS = 1e-6
ROPE_THETA = 10000.0

LANES = 128
SUBLANES = 8
VMEM_LIMIT_BYTES = 56 * 1024 * 1024

HGRN_HEAD = 128
HGRN_CHUNK = 64
HGRN_SUB = 8
MLA_NOPE = 128
MLA_ROPE = 64
MLA_QK = MLA_NOPE + MLA_ROPE
MLA_V = 128
MLA_QK_PAD = 2 * LANES
FFN_CONV = 3
ATTN_ROW_GROUP = 64
ATTN_UNROLL = 4
NEG_BIG = -1e30
MLA_Q_SCALE = (MLA_QK ** -0.5) * math.log2(math.e)
MLA_SCORE_MAX = 60.0

_NT = (((1,), (1,)), ((), ()))


def _compiler_params(semantics):
    return pltpu.CompilerParams(dimension_semantics=semantics, vmem_limit_bytes=VMEM_LIMIT_BYTES)


def _resident(shape, index_map):
    return pl.BlockSpec(shape, index_map, pipeline_mode=pl.Buffered(1))


def _rms_scale(x, n):
    return lax.rsqrt(jnp.sum(x * x, axis=-1, keepdims=True) * (1.0 / n) + RMS_EPS)


def _sigmoid(x):
    return 1.0 / (1.0 + jnp.exp(-x))


def _hgrn_in_kernel(x_ref, g_ref, w_ref, lb_ref, q_ref, lf_ref, v_ref, gate_ref):
    x = x_ref[...]
    d = x.shape[-1]
    h = (x * _rms_scale(x, d) * g_ref[...]).astype(BF16)
    f, q, g, i = (jnp.dot(h, w_ref[:, n * d:(n + 1) * d], preferred_element_type=F32) for n in (1, 0, 3, 2))
    lb = lb_ref[...]
    forget = lb + (1.0 - lb) * _sigmoid(f)
    lf_ref[...] = jnp.log2(forget)
    q_ref[...] = q * _sigmoid(q)
    gate_ref[...] = g * _sigmoid(g)
    v_ref[...] = i.astype(BF16)


def _hgrn_in(x2d, gain, w_in, lb, tm):
    m, d = x2d.shape
    row = pl.BlockSpec((tm, d), lambda i: (i, 0))
    vec = _resident((1, d), lambda i: (0, 0))
    return pl.pallas_call(
        _hgrn_in_kernel,
        grid=(m // tm,),
        in_specs=[row, vec, _resident((d, 4 * d), lambda i: (0, 0)), vec],
        out_specs=[row, row, row, row],
        out_shape=[
            jax.ShapeDtypeStruct((m, d), F32),
            jax.ShapeDtypeStruct((m, d), F32),
            jax.ShapeDtypeStruct((m, d), BF16),
            jax.ShapeDtypeStruct((m, d), F32),
        ],
        compiler_params=_compiler_params(("arbitrary",)),
        name="hgrn_in",
    )(x2d, gain, w_in, lb)


def _cumsum_rows(x):
    c = x.shape[0]
    row = lax.broadcasted_iota(jnp.int32, (c, c), 0)
    col = lax.broadcasted_iota(jnp.int32, (c, c), 1)
    tri = jnp.where(col <= row, 1.0, 0.0).astype(BF16)
    hi = x.astype(BF16)
    rem = x - hi.astype(F32)
    mid = rem.astype(BF16)
    lo = (rem - mid.astype(F32)).astype(BF16)
    out = jnp.dot(tri, hi, preferred_element_type=F32)
    out = out + jnp.dot(tri, mid, preferred_element_type=F32)
    return out + jnp.dot(tri, lo, preferred_element_type=F32)


def _hgrn_offdiag(q, k, g_cum):
    c, sub, half = HGRN_CHUNK, HGRN_SUB, SUBLANES
    rows = [jnp.zeros((half, c), F32) for _ in range(sub // half)]
    for j in range(1, c // sub):
        r0 = j * sub
        g_ref = g_cum[r0 - 1:r0, :]
        q_off = (q[r0:r0 + sub] * jnp.exp2(g_cum[r0:r0 + sub] - g_ref)).astype(BF16)
        k_off = k[:r0] * jnp.exp2(jnp.minimum(g_ref - g_cum[:r0], 0.0))
        k_off = jnp.concatenate([k_off, jnp.zeros((c - r0, HGRN_HEAD), F32)], axis=0).astype(BF16)
        a_off = lax.dot_general(q_off, k_off, _NT, preferred_element_type=F32)
        rows += [a_off[g * half:(g + 1) * half] for g in range(sub // half)]
    return rows


def _hgrn_diag(q, h_cum, g_cum, a_rows):
    c, sub, half = HGRN_CHUNK, HGRN_SUB, SUBLANES
    lane = lax.broadcasted_iota(jnp.int32, (half, c), 1)
    sub_row = lax.broadcasted_iota(jnp.int32, (half, HGRN_HEAD), 0)
    groups = sub // half
    a_rows = list(a_rows)
    for j in range(c // sub):
        r0 = j * sub
        for s in range(sub):
            hs = h_cum[r0 + s:r0 + s + 1]
            for g in range(s // half, groups):
                rows = slice(r0 + g * half, r0 + (g + 1) * half)
                diff = g_cum[rows] - hs
                if g == s // half:
                    diff = jnp.where(sub_row >= s - g * half, diff, NEG_BIG)
                col = jnp.sum(q[rows] * jnp.exp2(diff), axis=-1, keepdims=True)
                a_rows[j * groups + g] = jnp.where(lane == r0 + s, col, a_rows[j * groups + g])
    return jnp.concatenate(a_rows, axis=0).astype(BF16)


def _hgrn_rec_kernel(q_ref, lf_ref, v_ref, gate_ref, gain_ref, o_ref, state_ref):
    @pl.when(pl.program_id(2) == 0)
    def _():
        state_ref[...] = jnp.zeros_like(state_ref)

    c = HGRN_CHUNK
    gain = gain_ref[...]
    n_heads = q_ref.shape[2] // HGRN_HEAD
    n_chunks = q_ref.shape[1] // c
    tiles = [(slice(ci * c, (ci + 1) * c), slice(hd * HGRN_HEAD, (hd + 1) * HGRN_HEAD))
             for hd in range(n_heads) for ci in range(n_chunks)]
    n = len(tiles)

    q = [q_ref[0, rows, cols] for rows, cols in tiles]
    lf = [lf_ref[0, rows, cols] for rows, cols in tiles]
    v = [v_ref[0, rows, cols] for rows, cols in tiles]
    g_cum = [_cumsum_rows(x) for x in lf]
    k = [jnp.maximum(1.0 - jnp.exp2(x), 0.0) for x in lf]
    a_rows = [_hgrn_offdiag(q[i], k[i], g_cum[i]) for i in range(n)]
    g_last = [g[c - 1:c, :] for g in g_cum]
    k_dec = [(k[i] * jnp.exp2(g_last[i] - g_cum[i])).astype(BF16) for i in range(n)]
    update = [jnp.dot(v[i].astype(F32).T.astype(BF16), k_dec[i], preferred_element_type=F32) for i in range(n)]
    q_dec = [(q[i] * jnp.exp2(g_cum[i])).astype(BF16) for i in range(n)]
    h_cum = [g_cum[i] - jnp.log2(k[i]) for i in range(n)]
    o_intra = [jnp.dot(_hgrn_diag(q[i], h_cum[i], g_cum[i], a_rows[i]), v[i], preferred_element_type=F32)
               for i in range(n)]

    for hd in range(n_heads):
        state_t = state_ref[hd]
        for ci in range(n_chunks):
            i = hd * n_chunks + ci
            rows, cols = tiles[i]
            o = o_intra[i] + lax.dot_general(q_dec[i], state_t.astype(BF16), _NT, preferred_element_type=F32)
            state_t = state_t * jnp.exp2(g_last[i]) + update[i]
            y = o * _rms_scale(o, HGRN_HEAD) * gain
            o_ref[0, rows, cols] = (y * gate_ref[0, rows, cols]).astype(BF16)
        state_ref[hd] = state_t


def _hgrn_rec(q, lf, v, gate, gain, tt, hg):
    b, s, d = q.shape
    heads = d // HGRN_HEAD
    blk = pl.BlockSpec((1, tt, hg * HGRN_HEAD), lambda bi, hi, ti: (bi, ti, hi))
    return pl.pallas_call(
        _hgrn_rec_kernel,
        grid=(b, heads // hg, s // tt),
        in_specs=[blk, blk, blk, blk, pl.BlockSpec((1, HGRN_HEAD), lambda bi, hi, ti: (0, 0))],
        out_specs=blk,
        out_shape=jax.ShapeDtypeStruct((b, s, d), BF16),
        scratch_shapes=[pltpu.VMEM((hg, HGRN_HEAD, HGRN_HEAD), F32)],
        compiler_params=_compiler_params(("arbitrary", "arbitrary", "arbitrary")),
        name="hgrn_rec",
    )(q, lf, v, gate, gain)


def _rope(t, cos_t, sin_t):
    return t * cos_t + pltpu.roll(t, LANES // 2, axis=1) * sin_t


def _mla_in_kernel(x_ref, cos_ref, sin_ref, cost_ref, sint_ref, g_ref, w_in_ref, qa_ref, w_q_ref, kva_ref,
                   w_knt_ref, w_v_ref, qn_ref, kn_nope_ref, kn_rope_ref, q_out, kt_out, v_out,
                   *, q_lora, kv_lora):
    x = x_ref[0]
    tm, d = x.shape
    heads = q_out.shape[1]
    quarter = LANES // 4
    h = (x * _rms_scale(x, d) * g_ref[...]).astype(BF16)
    c = jnp.dot(h, w_in_ref[...], preferred_element_type=F32)
    c_q = c[:, :q_lora]
    c_kv = c[:, q_lora:q_lora + kv_lora]
    k_rope = c[:, q_lora + kv_lora:]
    c_q = (c_q * _rms_scale(c_q, q_lora) * qa_ref[...]).astype(BF16)
    c_kv = (c_kv * _rms_scale(c_kv, kv_lora) * kva_ref[...]).astype(BF16)
    q = jnp.dot(c_q, w_q_ref[...], preferred_element_type=F32)
    v_all = jnp.dot(c_kv, w_v_ref[...], preferred_element_type=F32)
    kn_t = lax.dot_general(w_knt_ref[...], c_kv, _NT, preferred_element_type=F32)

    q_tiles = [q[:, hd * MLA_QK_PAD:(hd + 1) * MLA_QK_PAD] for hd in range(heads)]
    q_sq = jnp.concatenate([(t * t).astype(BF16) for t in q_tiles], axis=0)
    ss_q = jnp.dot(q_sq, jnp.ones((MLA_QK_PAD, LANES), BF16), preferred_element_type=F32)
    r_q = lax.rsqrt(ss_q * (1.0 / MLA_QK) + RMS_EPS)
    cos_t = cos_ref[0]
    sin_t = sin_ref[0]
    qn = qn_ref[...]
    for hd in range(heads):
        r = r_q[hd * tm:(hd + 1) * tm]
        t = q_tiles[hd]
        q_out[0, hd, :, :LANES] = (t[:, :LANES] * r * qn[:, :LANES]).astype(BF16)
        q_out[0, hd, :, LANES:] = _rope(t[:, LANES:] * r * qn[:, LANES:], cos_t, sin_t).astype(BF16)

    kr_t = k_rope.T
    ss_r = jnp.sum(kr_t * kr_t, axis=0, keepdims=True)
    kr_g = kr_t * kn_rope_ref[...]
    x1 = kr_g[0:quarter]
    x2 = kr_g[2 * quarter:3 * quarter]
    cos_tt = cost_ref[0]
    sin_tt = sint_ref[0]
    zero = jnp.zeros((quarter, tm), F32)
    kr_rot = jnp.concatenate([x1 * cos_tt - x2 * sin_tt, zero, x2 * cos_tt + x1 * sin_tt, zero], axis=0)
    kn_gain = kn_nope_ref[...]
    ones = jnp.ones((tm, LANES), BF16)
    for hd in range(heads):
        kn = kn_t[hd * LANES:(hd + 1) * LANES]
        ss_k = jnp.sum(kn * kn, axis=0, keepdims=True) + ss_r
        r_k = lax.rsqrt(ss_k * (1.0 / MLA_QK) + RMS_EPS)
        kt_out[0, hd, :LANES, :] = (kn * r_k * kn_gain).astype(BF16)
        kt_out[0, hd, LANES:, :] = (kr_rot * r_k).astype(BF16)
        v_out[0, hd, :, :MLA_V] = v_all[:, hd * MLA_V:(hd + 1) * MLA_V].astype(BF16)
        v_out[0, hd, :, MLA_V:] = ones


def _mla_in(x, cos_t, sin_t, cos_tt, sin_tt, gain, w_in, qa, w_q, kva, w_knt, w_v, qn, kn_nope, kn_rope,
            heads, tm):
    b, s, d = x.shape
    q_lora = qa.shape[1]
    kv_lora = kva.shape[1]
    const = lambda bi, si: (0, 0)
    row = lambda w: pl.BlockSpec((1, tm, w), lambda bi, si: (bi, si, 0))
    col = pl.BlockSpec((1, cos_tt.shape[1], tm), lambda bi, si: (bi, 0, si))
    head_out = lambda w: pl.BlockSpec((1, heads, tm, w), lambda bi, si: (bi, 0, si, 0))
    return pl.pallas_call(
        functools.partial(_mla_in_kernel, q_lora=q_lora, kv_lora=kv_lora),
        grid=(b, s // tm),
        in_specs=[
            row(d), row(LANES), row(LANES), col, col,
            _resident((1, d), const),
            _resident(w_in.shape, const),
            _resident(qa.shape, const),
            _resident(w_q.shape, const),
            _resident(kva.shape, const),
            _resident(w_knt.shape, const),
            _resident(w_v.shape, const),
            _resident(qn.shape, const),
            _resident(kn_nope.shape, const),
            _resident(kn_rope.shape, const),
        ],
        out_specs=[
            head_out(MLA_QK_PAD),
            pl.BlockSpec((1, heads, MLA_QK_PAD, tm), lambda bi, si: (bi, 0, 0, si)),
            head_out(MLA_V + LANES),
        ],
        out_shape=[
            jax.ShapeDtypeStruct((b, heads, s, MLA_QK_PAD), BF16),
            jax.ShapeDtypeStruct((b, heads, MLA_QK_PAD, s), BF16),
            jax.ShapeDtypeStruct((b, heads, s, MLA_V + LANES), BF16),
        ],
        compiler_params=_compiler_params(("arbitrary", "arbitrary")),
        name="mla_in",
    )(x, cos_t, sin_t, cos_tt, sin_tt, gain, w_in, qa, w_q, kva, w_knt, w_v, qn, kn_nope, kn_rope)


def _attn_online_kernel(q_ref, kt_ref, v_ref, o_ref, acc_ref, m_ref, alpha_ref, s0_ref, s1_ref, p_ref,
                        *, tq):
    qi = pl.program_id(2)
    hg = q_ref.shape[1]
    dv = v_ref.shape[-1] - LANES
    lane_tiles = tq // LANES

    def scores(ki, s_ref):
        cols = pl.ds(pl.multiple_of(ki * tq, tq), tq)
        for hd in range(hg):
            s_ref[hd] = jnp.dot(q_ref[0, hd], kt_ref[0, hd, :, cols], preferred_element_type=F32)

    def consume(ki, s_ref, masked):
        rows = pl.ds(pl.multiple_of(ki * tq, tq), tq)
        for hd in range(hg):
            for g in range(tq // ATTN_ROW_GROUP):
                rg = slice(g * ATTN_ROW_GROUP, (g + 1) * ATTN_ROW_GROUP)
                s = s_ref[hd, rg, :]
                if masked:
                    r = lax.broadcasted_iota(jnp.int32, s.shape, 0) + g * ATTN_ROW_GROUP
                    c = lax.broadcasted_iota(jnp.int32, s.shape, 1)
                    s = jnp.where(c <= r, s, NEG_BIG)
                m_prev = m_ref[hd, rg, :]
                m_new = jnp.maximum(m_prev, jnp.max(s, axis=-1, keepdims=True))
                alpha = jnp.exp2(m_prev - m_new)
                m_ref[hd, rg, :] = m_new
                alpha_ref[hd, rg, :LANES] = alpha
                alpha_ref[hd, rg, LANES:] = alpha
                m_wide = jnp.concatenate([m_new] * lane_tiles, axis=1)
                p_ref[hd, rg, :] = jnp.exp2(s - m_wide).astype(BF16)
            pv = jnp.dot(p_ref[hd], v_ref[0, hd, rows, :], preferred_element_type=F32)
            acc_ref[hd] = alpha_ref[hd] * acc_ref[hd] + pv

    def pair(kp, carry):
        k0 = 2 * kp
        scores(k0 + 1, s1_ref)
        consume(k0, s0_ref, False)
        scores(k0 + 2, s0_ref)
        consume(k0 + 1, s1_ref, False)
        return carry

    acc_ref[...] = jnp.zeros_like(acc_ref)
    m_ref[...] = jnp.full(m_ref.shape, NEG_BIG, F32)
    scores(0, s0_ref)
    lax.fori_loop(0, qi // 2, pair, 0)

    @pl.when(qi % 2 == 0)
    def _():
        consume(qi, s0_ref, True)

    @pl.when(qi % 2 == 1)
    def _():
        scores(qi, s1_ref)
        consume(qi - 1, s0_ref, False)
        consume(qi, s1_ref, True)

    for hd in range(hg):
        o_ref[0, :, hd * dv:(hd + 1) * dv] = (acc_ref[hd, :, :dv] / acc_ref[hd, :, dv:]).astype(o_ref.dtype)


def _attn_bounded_kernel(q_ref, kt_ref, v_ref, o_ref, acc_ref, pd_ref, *p_refs, tq):
    qi = pl.program_id(2)
    hg = q_ref.shape[1]
    dv = v_ref.shape[-1] - LANES
    unroll = len(p_refs)

    def probs(ki, p_ref, masked):
        cols = pl.ds(pl.multiple_of(ki * tq, tq), tq)
        for hd in range(hg):
            s = jnp.dot(q_ref[0, hd], kt_ref[0, hd, :, cols], preferred_element_type=F32)
            if masked:
                r = lax.broadcasted_iota(jnp.int32, s.shape, 0)
                c = lax.broadcasted_iota(jnp.int32, s.shape, 1)
                s = jnp.where(c <= r, s, NEG_BIG)
            p_ref[hd] = jnp.exp2(s).astype(BF16)

    def accumulate(ki, p_ref):
        rows = pl.ds(pl.multiple_of(ki * tq, tq), tq)
        for hd in range(hg):
            acc_ref[hd] += jnp.dot(p_ref[hd], v_ref[0, hd, rows, :], preferred_element_type=F32)

    def steps(k0, count):
        for i in range(count):
            probs(k0 + i + 1, p_refs[(i + 1) % unroll], False)
            accumulate(k0 + i, p_refs[i])

    def group(kg, carry):
        steps(unroll * kg, unroll)
        return carry

    acc_ref[...] = jnp.zeros_like(acc_ref)

    @pl.when(qi == 0)
    def _():
        probs(0, pd_ref, True)
        accumulate(0, pd_ref)

    @pl.when(qi > 0)
    def _():
        probs(0, p_refs[0], False)
        n_steps = qi - 1
        lax.fori_loop(0, n_steps // unroll, group, 0)
        done = (n_steps // unroll) * unroll
        for rem in range(unroll):
            @pl.when(n_steps % unroll == rem)
            def _(rem=rem):
                steps(done, rem)
                probs(qi, pd_ref, True)
                accumulate(qi - 1, p_refs[rem])
                accumulate(qi, pd_ref)

    for hd in range(hg):
        o_ref[0, :, hd * dv:(hd + 1) * dv] = (acc_ref[hd, :, :dv] / acc_ref[hd, :, dv:]).astype(o_ref.dtype)


def _attention(q, kt, v, tq, hg, bounded):
    b, heads, s, dq = q.shape
    dve = v.shape[-1]
    dv = dve - LANES
    acc = pltpu.VMEM((hg, tq, dve), F32)
    probs = pltpu.VMEM((hg, tq, tq), BF16)
    scores = pltpu.VMEM((hg, tq, tq), F32)
    if bounded:
        body, scratch = _attn_bounded_kernel, [acc, probs] + [probs] * ATTN_UNROLL
    else:
        running_max = pltpu.VMEM((hg, tq, LANES), F32)
        rescale = pltpu.VMEM((hg, tq, dve), F32)
        body, scratch = _attn_online_kernel, [acc, running_max, rescale, scores, scores, probs]
    return pl.pallas_call(
        functools.partial(body, tq=tq),
        grid=(b, heads // hg, s // tq),
        in_specs=[
            pl.BlockSpec((1, hg, tq, dq), lambda bi, hi, qi: (bi, hi, qi, 0)),
            pl.BlockSpec((1, hg, dq, s), lambda bi, hi, qi: (bi, hi, 0, 0)),
            pl.BlockSpec((1, hg, s, dve), lambda bi, hi, qi: (bi, hi, 0, 0)),
        ],
        out_specs=pl.BlockSpec((1, tq, hg * dv), lambda bi, hi, qi: (bi, qi, hi)),
        out_shape=jax.ShapeDtypeStruct((b, s, heads * dv), BF16),
        scratch_shapes=scratch,
        compiler_params=_compiler_params(("arbitrary", "arbitrary", "arbitrary")),
        name="mla_attention_bounded" if bounded else "mla_attention_online",
    )(q, kt, v)


def _ffn_kernel(x_ref, a_ref, w_o_ref, g_ref, w_up_ref, cw_ref, cb_ref, w_down_ref, o_ref, tail_ref, stage_ref,
                *, d_ff, fc):
    @pl.when(pl.program_id(1) == 0)
    def _():
        tail_ref[...] = jnp.zeros_like(tail_ref)

    x1 = x_ref[0] + jnp.dot(a_ref[0], w_o_ref[...], preferred_element_type=F32)
    tm, d = x1.shape
    h = (x1 * _rms_scale(x1, d) * g_ref[...]).astype(BF16)
    n_chunks = d_ff // fc

    for p in range(2 * n_chunks):
        cols = slice(p * fc, (p + 1) * fc)
        u = jnp.dot(h, w_up_ref[:, cols], preferred_element_type=F32)
        stage_ref[p, 0:SUBLANES, :] = tail_ref[:, cols]
        stage_ref[p, SUBLANES:SUBLANES + tm, :] = u
        tail_ref[:, cols] = u[tm - SUBLANES:tm, :]

    def conv(p):
        cols = slice(p * fc, (p + 1) * fc)
        w = cw_ref[:, cols]
        u = stage_ref[p, SUBLANES:SUBLANES + tm, :]
        u1 = stage_ref[p, SUBLANES - 1:SUBLANES - 1 + tm, :]
        u2 = stage_ref[p, SUBLANES - 2:SUBLANES - 2 + tm, :]
        return cb_ref[:, cols] + u2 * w[0:1] + u1 * w[1:2] + u * w[2:3]

    acts = []
    for j in range(n_chunks):
        gate = conv(j)
        acts.append((gate * _sigmoid(gate) * conv(n_chunks + j)).astype(BF16))

    acc = x1
    for j in range(n_chunks):
        acc = acc + jnp.dot(acts[j], w_down_ref[j * fc:(j + 1) * fc, :], preferred_element_type=F32)
    o_ref[0] = acc


def _ffn(x, a, w_o, gain, w_up, conv_w, conv_b, w_down, tm, fc):
    b, s, d = x.shape
    d_ff = w_down.shape[0]
    const = lambda bi, si: (0, 0)
    row = pl.BlockSpec((1, tm, d), lambda bi, si: (bi, si, 0))
    return pl.pallas_call(
        functools.partial(_ffn_kernel, d_ff=d_ff, fc=fc),
        grid=(b, s // tm),
        in_specs=[
            row,
            pl.BlockSpec((1, tm, a.shape[-1]), lambda bi, si: (bi, si, 0)),
            _resident(w_o.shape, const),
            _resident((1, d), const),
            _resident(w_up.shape, const),
            _resident(conv_w.shape, const),
            _resident(conv_b.shape, const),
            _resident(w_down.shape, const),
        ],
        out_specs=row,
        out_shape=jax.ShapeDtypeStruct((b, s, d), F32),
        scratch_shapes=[
            pltpu.VMEM((SUBLANES, 2 * d_ff), F32),
            pltpu.VMEM((2 * d_ff // fc, SUBLANES + tm, fc), F32),
        ],
        compiler_params=_compiler_params(("arbitrary", "arbitrary")),
        name="conv_ffn",
    )(x, a, w_o, gain, w_up, conv_w, conv_b, w_down)


def _spread_rope(w):
    half = MLA_ROPE // 2
    z = jnp.zeros(w.shape[:-1] + (LANES // 2 - half,), w.dtype)
    return jnp.concatenate([w[..., :half], z, w[..., half:], z], axis=-1)


def _pad_qk_channels(w, heads):
    w = w.reshape(w.shape[:-1] + (heads, MLA_QK))
    w = jnp.concatenate([w[..., :MLA_NOPE], _spread_rope(w[..., MLA_NOPE:])], axis=-1)
    return w.reshape(w.shape[:-2] + (heads * MLA_QK_PAD,))


def kernel(x, positions, norm_mix, norm_ffn, hgrn_w_in, hgrn_lower_bounds, hgrn_out_norm, hgrn_w_out,
           mla_w_in, mla_q_a_norm, mla_w_q_up, mla_kv_a_norm, mla_w_kv_up, mla_q_norm, mla_k_norm,
           mla_w_out, ffn_w_up, ffn_conv_w, ffn_conv_b, ffn_w_down):
    b, s, d = x.shape
    depth = norm_mix.shape[0]
    n_mixers = 2
    q_lora = mla_q_a_norm.shape[1]
    kv_lora = mla_kv_a_norm.shape[1]
    heads = mla_w_q_up.shape[2] // MLA_QK
    mla_tm = 512

    lb_soft = jax.nn.softmax(hgrn_lower_bounds.astype(F32), axis=0)
    lower_bounds = jnp.cumsum(lb_soft, axis=0) - lb_soft[0:1]
    inv_freq = ROPE_THETA ** (-jnp.arange(0, MLA_ROPE, 2, dtype=F32) / MLA_ROPE)
    ang = positions.astype(F32)[..., None] * inv_freq
    cos, sin = jnp.cos(ang), jnp.sin(ang)
    zeros = jnp.zeros_like(cos)
    cos_t = jnp.concatenate([cos, zeros, cos, zeros], axis=-1)
    sin_t = jnp.concatenate([-sin, zeros, sin, zeros], axis=-1)
    cos_tt = jnp.swapaxes(cos, 1, 2)
    sin_tt = jnp.swapaxes(sin, 1, 2)

    x = x.astype(F32)
    for layer in range(depth):
        j = layer // n_mixers
        gain = norm_mix[layer].reshape(1, d).astype(F32)
        if layer % n_mixers == 0:
            q, lf, v, gate = _hgrn_in(
                x.reshape(b * s, d), gain, hgrn_w_in[j].astype(BF16),
                lower_bounds[j].reshape(1, d), tm=512)
            shp = (b, s, d)
            o = _hgrn_rec(q.reshape(shp), lf.reshape(shp), v.reshape(shp), gate.reshape(shp),
                          hgrn_out_norm[j].reshape(1, HGRN_HEAD).astype(F32), tt=512, hg=2)
            w_out = hgrn_w_out[j]
        else:
            w_in = jnp.concatenate(
                [mla_w_in[j][:, :q_lora + kv_lora], _spread_rope(mla_w_in[j][:, q_lora + kv_lora:])], axis=-1)
            w_kv = mla_w_kv_up[j].reshape(kv_lora, heads, MLA_NOPE + MLA_V)
            w_knt = jnp.transpose(w_kv[..., :MLA_NOPE], (1, 2, 0)).reshape(heads * MLA_NOPE, kv_lora)
            w_v = w_kv[..., MLA_NOPE:].reshape(kv_lora, heads * MLA_V)
            q_gain = _pad_qk_channels(mla_q_norm[j].reshape(1, MLA_QK), 1).astype(F32) * MLA_Q_SCALE
            k_gain = _pad_qk_channels(mla_k_norm[j].reshape(1, MLA_QK), 1).astype(F32)
            k_gain_t = jnp.broadcast_to(k_gain.reshape(MLA_QK_PAD, 1), (MLA_QK_PAD, mla_tm))
            bound = 1.02 * MLA_QK * MLA_Q_SCALE * jnp.max(jnp.abs(mla_q_norm[j])) * jnp.max(jnp.abs(mla_k_norm[j]))
            qh, kth, vh = _mla_in(
                x, cos_t, sin_t, cos_tt, sin_tt, gain, w_in.astype(BF16),
                mla_q_a_norm[j].reshape(1, q_lora).astype(F32),
                _pad_qk_channels(mla_w_q_up[j], heads).astype(BF16),
                mla_kv_a_norm[j].reshape(1, kv_lora).astype(F32),
                w_knt.astype(BF16), w_v.astype(BF16),
                q_gain, k_gain_t[:LANES], k_gain_t[LANES:],
                heads=heads, tm=mla_tm)
            o = lax.cond(
                bound <= MLA_SCORE_MAX,
                functools.partial(_attention, tq=512, hg=2, bounded=True),
                functools.partial(_attention, tq=512, hg=2, bounded=False),
                qh, kth, vh)
            w_out = mla_w_out[j]
        x = _ffn(x, o, w_out.astype(BF16), norm_ffn[layer].reshape(1, d).astype(F32),
                 ffn_w_up[layer].astype(BF16), ffn_conv_w[layer].astype(F32),
                 ffn_conv_b[layer].reshape(1, -1).astype(F32), ffn_w_down[layer].astype(BF16),
                 tm=512, fc=256)
    return x
```

```python
import functools
import math

import jax
import jax.numpy as jnp
from jax import lax
from jax.experimental import pallas as pl
from jax.experimental.pallas import tpu as pltpu

F32 = jnp.float32
BF16 = jnp.bfloat16

RMS_EPS = 1e-6
ROPE_THETA = 10000.0

LANES = 128
SUBLANES = 8
VMEM_LIMIT_BYTES = 56 * 1024 * 1024

HGRN_HEAD = 128
HGRN_CHUNK = 64
HGRN_SUB = 8
MLA_NOPE = 128
MLA_ROPE = 64
MLA_QK = MLA_NOPE + MLA_ROPE
MLA_V = 128
MLA_QK_PAD = 2 * LANES
FFN_CONV = 3
ATTN_ROW_GROUP = 64
ATTN_UNROLL = 8
NEG_BIG = -1e30
MLA_Q_SCALE = (MLA_QK ** -0.5) * math.log2(math.e)
MLA_SCORE_MAX = 60.0

_NT = (((1,), (1,)), ((), ()))


def _compiler_params(semantics):
    return pltpu.CompilerParams(dimension_semantics=semantics, vmem_limit_bytes=VMEM_LIMIT_BYTES)


def _resident(shape, index_map):
    return pl.BlockSpec(shape, index_map, pipeline_mode=pl.Buffered(1))


def _rms_scale(x, n):
    return lax.rsqrt(jnp.sum(x * x, axis=-1, keepdims=True) * (1.0 / n) + RMS_EPS)


def _sigmoid(x):
    return 1.0 / (1.0 + jnp.exp(-x))


def _hgrn_in_kernel(x_ref, g_ref, w_ref, lb_ref, q_ref, lf_ref, v_ref, gate_ref):
    x = x_ref[...]
    d = x.shape[-1]
    h = (x * _rms_scale(x, d) * g_ref[...]).astype(BF16)
    f, q, g, i = (jnp.dot(h, w_ref[:, n * d:(n + 1) * d], preferred_element_type=F32) for n in (1, 0, 3, 2))
    lb = lb_ref[...]
    forget = lb + (1.0 - lb) * _sigmoid(f)
    lf_ref[...] = jnp.log2(forget)
    q_ref[...] = q * _sigmoid(q)
    gate_ref[...] = g * _sigmoid(g)
    v_ref[...] = i.astype(BF16)


def _hgrn_in(x2d, gain, w_in, lb, tm):
    m, d = x2d.shape
    row = pl.BlockSpec((tm, d), lambda i: (i, 0))
    vec = _resident((1, d), lambda i: (0, 0))
    return pl.pallas_call(
        _hgrn_in_kernel,
        grid=(m // tm,),
        in_specs=[row, vec, _resident((d, 4 * d), lambda i: (0, 0)), vec],
        out_specs=[row, row, row, row],
        out_shape=[
            jax.ShapeDtypeStruct((m, d), F32),
            jax.ShapeDtypeStruct((m, d), F32),
            jax.ShapeDtypeStruct((m, d), BF16),
            jax.ShapeDtypeStruct((m, d), F32),
        ],
        compiler_params=_compiler_params(("arbitrary",)),
        name="hgrn_in",
    )(x2d, gain, w_in, lb)


def _cumsum_rows(x):
    c = x.shape[0]
    row = lax.broadcasted_iota(jnp.int32, (c, c), 0)
    col = lax.broadcasted_iota(jnp.int32, (c, c), 1)
    tri = jnp.where(col <= row, 1.0, 0.0).astype(BF16)
    hi = x.astype(BF16)
    rem = x - hi.astype(F32)
    mid = rem.astype(BF16)
    lo = (rem - mid.astype(F32)).astype(BF16)
    out = jnp.dot(tri, hi, preferred_element_type=F32)
    out = out + jnp.dot(tri, mid, preferred_element_type=F32)
    return out + jnp.dot(tri, lo, preferred_element_type=F32)


def _hgrn_offdiag(q, k, g_cum):
    c, sub, half = HGRN_CHUNK, HGRN_SUB, SUBLANES
    rows = [jnp.zeros((half, c), F32) for _ in range(sub // half)]
    for j in range(1, c // sub):
        r0 = j * sub
        g_ref = g_cum[r0 - 1:r0, :]
        q_off = (q[r0:r0 + sub] * jnp.exp2(g_cum[r0:r0 + sub] - g_ref)).astype(BF16)
        k_off = k[:r0] * jnp.exp2(jnp.minimum(g_ref - g_cum[:r0], 0.0))
        k_off = jnp.concatenate([k_off, jnp.zeros((c - r0, HGRN_HEAD), F32)], axis=0).astype(BF16)
        a_off = lax.dot_general(q_off, k_off, _NT, preferred_element_type=F32)
        rows += [a_off[g * half:(g + 1) * half] for g in range(sub // half)]
    return rows


def _hgrn_diag(q, h_cum, g_cum, a_rows):
    c, sub, half = HGRN_CHUNK, HGRN_SUB, SUBLANES
    lane = lax.broadcasted_iota(jnp.int32, (half, c), 1)
    sub_row = lax.broadcasted_iota(jnp.int32, (half, HGRN_HEAD), 0)
    groups = sub // half
    a_rows = list(a_rows)
    for j in range(c // sub):
        r0 = j * sub
        for s in range(sub):
            hs = h_cum[r0 + s:r0 + s + 1]
            for g in range(s // half, groups):
                rows = slice(r0 + g * half, r0 + (g + 1) * half)
                diff = g_cum[rows] - hs
                if g == s // half:
                    diff = jnp.where(sub_row >= s - g * half, diff, NEG_BIG)
                col = jnp.sum(q[rows] * jnp.exp2(diff), axis=-1, keepdims=True)
                a_rows[j * groups + g] = jnp.where(lane == r0 + s, col, a_rows[j * groups + g])
    return jnp.concatenate(a_rows, axis=0).astype(BF16)


def _hgrn_rec_kernel(q_ref, lf_ref, v_ref, gate_ref, gain_ref, o_ref, state_ref):
    @pl.when(pl.program_id(2) == 0)
    def _():
        state_ref[...] = jnp.zeros_like(state_ref)

    c = HGRN_CHUNK
    gain = gain_ref[...]
    n_heads = q_ref.shape[2] // HGRN_HEAD
    n_chunks = q_ref.shape[1] // c
    tiles = [(slice(ci * c, (ci + 1) * c), slice(hd * HGRN_HEAD, (hd + 1) * HGRN_HEAD))
             for hd in range(n_heads) for ci in range(n_chunks)]
    n = len(tiles)

    q = [q_ref[0, rows, cols] for rows, cols in tiles]
    lf = [lf_ref[0, rows, cols] for rows, cols in tiles]
    v = [v_ref[0, rows, cols] for rows, cols in tiles]
    g_cum = [_cumsum_rows(x) for x in lf]
    k = [jnp.maximum(1.0 - jnp.exp2(x), 0.0) for x in lf]
    a_rows = [_hgrn_offdiag(q[i], k[i], g_cum[i]) for i in range(n)]
    g_last = [g[c - 1:c, :] for g in g_cum]
    k_dec = [(k[i] * jnp.exp2(g_last[i] - g_cum[i])).astype(BF16) for i in range(n)]
    update = [jnp.dot(v[i].astype(F32).T.astype(BF16), k_dec[i], preferred_element_type=F32) for i in range(n)]
    q_dec = [(q[i] * jnp.exp2(g_cum[i])).astype(BF16) for i in range(n)]
    h_cum = [g_cum[i] - jnp.log2(k[i]) for i in range(n)]
    o_intra = [jnp.dot(_hgrn_diag(q[i], h_cum[i], g_cum[i], a_rows[i]), v[i], preferred_element_type=F32)
               for i in range(n)]

    for hd in range(n_heads):
        state_t = state_ref[hd]
        for ci in range(n_chunks):
            i = hd * n_chunks + ci
            rows, cols = tiles[i]
            o = o_intra[i] + lax.dot_general(q_dec[i], state_t.astype(BF16), _NT, preferred_element_type=F32)
            state_t = state_t * jnp.exp2(g_last[i]) + update[i]
            y = o * _rms_scale(o, HGRN_HEAD) * gain
            o_ref[0, rows, cols] = (y * gate_ref[0, rows, cols]).astype(BF16)
        state_ref[hd] = state_t


def _hgrn_rec(q, lf, v, gate, gain, tt, hg):
    b, s, d = q.shape
    heads = d // HGRN_HEAD
    blk = pl.BlockSpec((1, tt, hg * HGRN_HEAD), lambda bi, hi, ti: (bi, ti, hi))
    return pl.pallas_call(
        _hgrn_rec_kernel,
        grid=(b, heads // hg, s // tt),
        in_specs=[blk, blk, blk, blk, pl.BlockSpec((1, HGRN_HEAD), lambda bi, hi, ti: (0, 0))],
        out_specs=blk,
        out_shape=jax.ShapeDtypeStruct((b, s, d), BF16),
        scratch_shapes=[pltpu.VMEM((hg, HGRN_HEAD, HGRN_HEAD), F32)],
        compiler_params=_compiler_params(("arbitrary", "arbitrary", "arbitrary")),
        name="hgrn_rec",
    )(q, lf, v, gate, gain)


def _rope(t, cos_t, sin_t):
    return t * cos_t + pltpu.roll(t, LANES // 2, axis=1) * sin_t


def _mla_in_kernel(x_ref, cos_ref, sin_ref, cost_ref, sint_ref, g_ref, w_in_ref, qa_ref, w_q_ref, kva_ref,
                   w_knt_ref, w_v_ref, qn_ref, kn_nope_ref, kn_rope_ref, q_out, kt_out, v_out,
                   *, q_lora, kv_lora):
    x = x_ref[0]
    tm, d = x.shape
    heads = q_out.shape[1]
    quarter = LANES // 4
    h = (x * _rms_scale(x, d) * g_ref[...]).astype(BF16)
    c = jnp.dot(h, w_in_ref[...], preferred_element_type=F32)
    c_q = c[:, :q_lora]
    c_kv = c[:, q_lora:q_lora + kv_lora]
    k_rope = c[:, q_lora + kv_lora:]
    c_q = (c_q * _rms_scale(c_q, q_lora) * qa_ref[...]).astype(BF16)
    c_kv = (c_kv * _rms_scale(c_kv, kv_lora) * kva_ref[...]).astype(BF16)
    q = jnp.dot(c_q, w_q_ref[...], preferred_element_type=F32)
    v_all = jnp.dot(c_kv, w_v_ref[...], preferred_element_type=F32)
    kn_t = lax.dot_general(w_knt_ref[...], c_kv, _NT, preferred_element_type=F32)

    q_tiles = [q[:, hd * MLA_QK_PAD:(hd + 1) * MLA_QK_PAD] for hd in range(heads)]
    q_sq = jnp.concatenate([(t * t).astype(BF16) for t in q_tiles], axis=0)
    ss_q = jnp.dot(q_sq, jnp.ones((MLA_QK_PAD, LANES), BF16), preferred_element_type=F32)
    r_q = lax.rsqrt(ss_q * (1.0 / MLA_QK) + RMS_EPS)
    cos_t = cos_ref[0]
    sin_t = sin_ref[0]
    qn = qn_ref[...]
    for hd in range(heads):
        r = r_q[hd * tm:(hd + 1) * tm]
        t = q_tiles[hd]
        q_out[0, hd, :, :LANES] = (t[:, :LANES] * r * qn[:, :LANES]).astype(BF16)
        q_out[0, hd, :, LANES:] = _rope(t[:, LANES:] * r * qn[:, LANES:], cos_t, sin_t).astype(BF16)

    kr_t = k_rope.T
    ss_r = jnp.sum(kr_t * kr_t, axis=0, keepdims=True)
    kr_g = kr_t * kn_rope_ref[...]
    x1 = kr_g[0:quarter]
    x2 = kr_g[2 * quarter:3 * quarter]
    cos_tt = cost_ref[0]
    sin_tt = sint_ref[0]
    zero = jnp.zeros((quarter, tm), F32)
    kr_rot = jnp.concatenate([x1 * cos_tt - x2 * sin_tt, zero, x2 * cos_tt + x1 * sin_tt, zero], axis=0)
    kn_gain = kn_nope_ref[...]
    ones = jnp.ones((tm, LANES), BF16)
    for hd in range(heads):
        kn = kn_t[hd * LANES:(hd + 1) * LANES]
        ss_k = jnp.sum(kn * kn, axis=0, keepdims=True) + ss_r
        r_k = lax.rsqrt(ss_k * (1.0 / MLA_QK) + RMS_EPS)
        kt_out[0, hd, :LANES, :] = (kn * r_k * kn_gain).astype(BF16)
        kt_out[0, hd, LANES:, :] = (kr_rot * r_k).astype(BF16)
        v_out[0, hd, :, :MLA_V] = v_all[:, hd * MLA_V:(hd + 1) * MLA_V].astype(BF16)
        v_out[0, hd, :, MLA_V:] = ones


def _mla_in(x, cos_t, sin_t, cos_tt, sin_tt, gain, w_in, qa, w_q, kva, w_knt, w_v, qn, kn_nope, kn_rope,
            heads, tm):
    b, s, d = x.shape
    q_lora = qa.shape[1]
    kv_lora = kva.shape[1]
    const = lambda bi, si: (0, 0)
    row = lambda w: pl.BlockSpec((1, tm, w), lambda bi, si: (bi, si, 0))
    col = pl.BlockSpec((1, cos_tt.shape[1], tm), lambda bi, si: (bi, 0, si))
    head_out = lambda w: pl.BlockSpec((1, heads, tm, w), lambda bi, si: (bi, 0, si, 0))
    return pl.pallas_call(
        functools.partial(_mla_in_kernel, q_lora=q_lora, kv_lora=kv_lora),
        grid=(b, s // tm),
        in_specs=[
            row(d), row(LANES), row(LANES), col, col,
            _resident((1, d), const),
            _resident(w_in.shape, const),
            _resident(qa.shape, const),
            _resident(w_q.shape, const),
            _resident(kva.shape, const),
            _resident(w_knt.shape, const),
            _resident(w_v.shape, const),
            _resident(qn.shape, const),
            _resident(kn_nope.shape, const),
            _resident(kn_rope.shape, const),
        ],
        out_specs=[
            head_out(MLA_QK_PAD),
            pl.BlockSpec((1, heads, MLA_QK_PAD, tm), lambda bi, si: (bi, 0, 0, si)),
            head_out(MLA_V + LANES),
        ],
        out_shape=[
            jax.ShapeDtypeStruct((b, heads, s, MLA_QK_PAD), BF16),
            jax.ShapeDtypeStruct((b, heads, MLA_QK_PAD, s), BF16),
            jax.ShapeDtypeStruct((b, heads, s, MLA_V + LANES), BF16),
        ],
        compiler_params=_compiler_params(("arbitrary", "arbitrary")),
        name="mla_in",
    )(x, cos_t, sin_t, cos_tt, sin_tt, gain, w_in, qa, w_q, kva, w_knt, w_v, qn, kn_nope, kn_rope)


def _attn_online_kernel(q_ref, kt_ref, v_ref, o_ref, acc_ref, m_ref, alpha_ref, s0_ref, s1_ref, p_ref,
                        *, tq):
    qi = pl.program_id(2)
    hg = q_ref.shape[1]
    dv = v_ref.shape[-1] - LANES
    lane_tiles = tq // LANES

    def scores(ki, s_ref):
        cols = pl.ds(pl.multiple_of(ki * tq, tq), tq)
        for hd in range(hg):
            s_ref[hd] = jnp.dot(q_ref[0, hd], kt_ref[0, hd, :, cols], preferred_element_type=F32)

    def consume(ki, s_ref, masked):
        rows = pl.ds(pl.multiple_of(ki * tq, tq), tq)
        for hd in range(hg):
            for g in range(tq // ATTN_ROW_GROUP):
                rg = slice(g * ATTN_ROW_GROUP, (g + 1) * ATTN_ROW_GROUP)
                s = s_ref[hd, rg, :]
                if masked:
                    r = lax.broadcasted_iota(jnp.int32, s.shape, 0) + g * ATTN_ROW_GROUP
                    c = lax.broadcasted_iota(jnp.int32, s.shape, 1)
                    s = jnp.where(c <= r, s, NEG_BIG)
                m_prev = m_ref[hd, rg, :]
                m_new = jnp.maximum(m_prev, jnp.max(s, axis=-1, keepdims=True))
                alpha = jnp.exp2(m_prev - m_new)
                m_ref[hd, rg, :] = m_new
                alpha_ref[hd, rg, :LANES] = alpha
                alpha_ref[hd, rg, LANES:] = alpha
                m_wide = jnp.concatenate([m_new] * lane_tiles, axis=1)
                p_ref[hd, rg, :] = jnp.exp2(s - m_wide).astype(BF16)
            pv = jnp.dot(p_ref[hd], v_ref[0, hd, rows, :], preferred_element_type=F32)
            acc_ref[hd] = alpha_ref[hd] * acc_ref[hd] + pv

    def pair(kp, carry):
        k0 = 2 * kp
        scores(k0 + 1, s1_ref)
        consume(k0, s0_ref, False)
        scores(k0 + 2, s0_ref)
        consume(k0 + 1, s1_ref, False)
        return carry

    acc_ref[...] = jnp.zeros_like(acc_ref)
    m_ref[...] = jnp.full(m_ref.shape, NEG_BIG, F32)
    scores(0, s0_ref)
    lax.fori_loop(0, qi // 2, pair, 0)

    @pl.when(qi % 2 == 0)
    def _():
        consume(qi, s0_ref, True)

    @pl.when(qi % 2 == 1)
    def _():
        scores(qi, s1_ref)
        consume(qi - 1, s0_ref, False)
        consume(qi, s1_ref, True)

    for hd in range(hg):
        o_ref[0, :, hd * dv:(hd + 1) * dv] = (acc_ref[hd, :, :dv] / acc_ref[hd, :, dv:]).astype(o_ref.dtype)


def _attn_bounded_kernel(q_ref, kt_ref, v_ref, o_ref, acc_ref, pd_ref, *p_refs, tq):
    qi = pl.program_id(2)
    hg = q_ref.shape[1]
    dv = v_ref.shape[-1] - LANES
    unroll = len(p_refs)

    def probs(ki, p_ref, masked):
        cols = pl.ds(pl.multiple_of(ki * tq, tq), tq)
        for hd in range(hg):
            s = jnp.dot(q_ref[0, hd], kt_ref[0, hd, :, cols], preferred_element_type=F32)
            if masked:
                r = lax.broadcasted_iota(jnp.int32, s.shape, 0)
                c = lax.broadcasted_iota(jnp.int32, s.shape, 1)
                s = jnp.where(c <= r, s, NEG_BIG)
            p_ref[hd] = jnp.exp2(s).astype(BF16)

    def accumulate(ki, p_ref):
        rows = pl.ds(pl.multiple_of(ki * tq, tq), tq)
        for hd in range(hg):
            acc_ref[hd] += jnp.dot(p_ref[hd], v_ref[0, hd, rows, :], preferred_element_type=F32)

    def steps(k0, count):
        for i in range(count):
            probs(k0 + i + 1, p_refs[(i + 1) % unroll], False)
            accumulate(k0 + i, p_refs[i])

    def group(kg, carry):
        steps(unroll * kg, unroll)
        return carry

    acc_ref[...] = jnp.zeros_like(acc_ref)

    @pl.when(qi == 0)
    def _():
        probs(0, pd_ref, True)
        accumulate(0, pd_ref)

    @pl.when(qi > 0)
    def _():
        probs(0, p_refs[0], False)
        n_steps = qi - 1
        lax.fori_loop(0, n_steps // unroll, group, 0)
        done = (n_steps // unroll) * unroll
        for rem in range(unroll):
            @pl.when(n_steps % unroll == rem)
            def _(rem=rem):
                steps(done, rem)
                probs(qi, pd_ref, True)
                accumulate(qi - 1, p_refs[rem])
                accumulate(qi, pd_ref)

    for hd in range(hg):
        o_ref[0, :, hd * dv:(hd + 1) * dv] = (acc_ref[hd, :, :dv] / acc_ref[hd, :, dv:]).astype(o_ref.dtype)


def _attention(q, kt, v, tq, hg, bounded):
    b, heads, s, dq = q.shape
    dve = v.shape[-1]
    dv = dve - LANES
    acc = pltpu.VMEM((hg, tq, dve), F32)
    probs = pltpu.VMEM((hg, tq, tq), BF16)
    scores = pltpu.VMEM((hg, tq, tq), F32)
    if bounded:
        body, scratch = _attn_bounded_kernel, [acc, probs] + [probs] * ATTN_UNROLL
    else:
        running_max = pltpu.VMEM((hg, tq, LANES), F32)
        rescale = pltpu.VMEM((hg, tq, dve), F32)
        body, scratch = _attn_online_kernel, [acc, running_max, rescale, scores, scores, probs]
    return pl.pallas_call(
        functools.partial(body, tq=tq),
        grid=(b, heads // hg, s // tq),
        in_specs=[
            pl.BlockSpec((1, hg, tq, dq), lambda bi, hi, qi: (bi, hi, qi, 0)),
            pl.BlockSpec((1, hg, dq, s), lambda bi, hi, qi: (bi, hi, 0, 0)),
            pl.BlockSpec((1, hg, s, dve), lambda bi, hi, qi: (bi, hi, 0, 0)),
        ],
        out_specs=pl.BlockSpec((1, tq, hg * dv), lambda bi, hi, qi: (bi, qi, hi)),
        out_shape=jax.ShapeDtypeStruct((b, s, heads * dv), BF16),
        scratch_shapes=scratch,
        compiler_params=_compiler_params(("arbitrary", "arbitrary", "arbitrary")),
        name="mla_attention_bounded" if bounded else "mla_attention_online",
    )(q, kt, v)


def _ffn_kernel(x_ref, a_ref, w_o_ref, g_ref, w_up_ref, cw_ref, cb_ref, w_down_ref, o_ref, tail_ref, stage_ref,
                *, d_ff, fc):
    @pl.when(pl.program_id(1) == 0)
    def _():
        tail_ref[...] = jnp.zeros_like(tail_ref)

    x1 = x_ref[0] + jnp.dot(a_ref[0], w_o_ref[...], preferred_element_type=F32)
    tm, d = x1.shape
    h = (x1 * _rms_scale(x1, d) * g_ref[...]).astype(BF16)
    n_chunks = d_ff // fc

    for p in range(2 * n_chunks):
        cols = slice(p * fc, (p + 1) * fc)
        u = jnp.dot(h, w_up_ref[:, cols], preferred_element_type=F32)
        stage_ref[p, 0:SUBLANES, :] = tail_ref[:, cols]
        stage_ref[p, SUBLANES:SUBLANES + tm, :] = u
        tail_ref[:, cols] = u[tm - SUBLANES:tm, :]

    def conv(p):
        cols = slice(p * fc, (p + 1) * fc)
        w = cw_ref[:, cols]
        u = stage_ref[p, SUBLANES:SUBLANES + tm, :]
        u1 = stage_ref[p, SUBLANES - 1:SUBLANES - 1 + tm, :]
        u2 = stage_ref[p, SUBLANES - 2:SUBLANES - 2 + tm, :]
        return cb_ref[:, cols] + u2 * w[0:1] + u1 * w[1:2] + u * w[2:3]

    acts = []
    for j in range(n_chunks):
        gate = conv(j)
        acts.append((gate * _sigmoid(gate) * conv(n_chunks + j)).astype(BF16))

    acc = x1
    for j in range(n_chunks):
        acc = acc + jnp.dot(acts[j], w_down_ref[j * fc:(j + 1) * fc, :], preferred_element_type=F32)
    o_ref[0] = acc


def _ffn(x, a, w_o, gain, w_up, conv_w, conv_b, w_down, tm, fc):
    b, s, d = x.shape
    d_ff = w_down.shape[0]
    const = lambda bi, si: (0, 0)
    row = pl.BlockSpec((1, tm, d), lambda bi, si: (bi, si, 0))
    return pl.pallas_call(
        functools.partial(_ffn_kernel, d_ff=d_ff, fc=fc),
        grid=(b, s // tm),
        in_specs=[
            row,
            pl.BlockSpec((1, tm, a.shape[-1]), lambda bi, si: (bi, si, 0)),
            _resident(w_o.shape, const),
            _resident((1, d), const),
            _resident(w_up.shape, const),
            _resident(conv_w.shape, const),
            _resident(conv_b.shape, const),
            _resident(w_down.shape, const),
        ],
        out_specs=row,
        out_shape=jax.ShapeDtypeStruct((b, s, d), F32),
        scratch_shapes=[
            pltpu.VMEM((SUBLANES, 2 * d_ff), F32),
            pltpu.VMEM((2 * d_ff // fc, SUBLANES + tm, fc), F32),
        ],
        compiler_params=_compiler_params(("arbitrary", "arbitrary")),
        name="conv_ffn",
    )(x, a, w_o, gain, w_up, conv_w, conv_b, w_down)


def _spread_rope(w):
    half = MLA_ROPE // 2
    z = jnp.zeros(w.shape[:-1] + (LANES // 2 - half,), w.dtype)
    return jnp.concatenate([w[..., :half], z, w[..., half:], z], axis=-1)


def _pad_qk_channels(w, heads):
    w = w.reshape(w.shape[:-1] + (heads, MLA_QK))
    w = jnp.concatenate([w[..., :MLA_NOPE], _spread_rope(w[..., MLA_NOPE:])], axis=-1)
    return w.reshape(w.shape[:-2] + (heads * MLA_QK_PAD,))


def kernel(x, positions, norm_mix, norm_ffn, hgrn_w_in, hgrn_lower_bounds, hgrn_out_norm, hgrn_w_out,
           mla_w_in, mla_q_a_norm, mla_w_q_up, mla_kv_a_norm, mla_w_kv_up, mla_q_norm, mla_k_norm,
           mla_w_out, ffn_w_up, ffn_conv_w, ffn_conv_b, ffn_w_down):
    b, s, d = x.shape
    depth = norm_mix.shape[0]
    n_mixers = 2
    q_lora = mla_q_a_norm.shape[1]
    kv_lora = mla_kv_a_norm.shape[1]
    heads = mla_w_q_up.shape[2] // MLA_QK
    mla_tm = 512

    lb_soft = jax.nn.softmax(hgrn_lower_bounds.astype(F32), axis=0)
    lower_bounds = jnp.cumsum(lb_soft, axis=0) - lb_soft[0:1]
    inv_freq = ROPE_THETA ** (-jnp.arange(0, MLA_ROPE, 2, dtype=F32) / MLA_ROPE)
    ang = positions.astype(F32)[..., None] * inv_freq
    cos, sin = jnp.cos(ang), jnp.sin(ang)
    zeros = jnp.zeros_like(cos)
    cos_t = jnp.concatenate([cos, zeros, cos, zeros], axis=-1)
    sin_t = jnp.concatenate([-sin, zeros, sin, zeros], axis=-1)
    cos_tt = jnp.swapaxes(cos, 1, 2)
    sin_tt = jnp.swapaxes(sin, 1, 2)

    x = x.astype(F32)
    for layer in range(depth):
        j = layer // n_mixers
        gain = norm_mix[layer].reshape(1, d).astype(F32)
        if layer % n_mixers == 0:
            q, lf, v, gate = _hgrn_in(
                x.reshape(b * s, d), gain, hgrn_w_in[j].astype(BF16),
                lower_bounds[j].reshape(1, d), tm=512)
            shp = (b, s, d)
            o = _hgrn_rec(q.reshape(shp), lf.reshape(shp), v.reshape(shp), gate.reshape(shp),
                          hgrn_out_norm[j].reshape(1, HGRN_HEAD).astype(F32), tt=512, hg=2)
            w_out = hgrn_w_out[j]
        else:
            w_in = jnp.concatenate(
                [mla_w_in[j][:, :q_lora + kv_lora], _spread_rope(mla_w_in[j][:, q_lora + kv_lora:])], axis=-1)
            w_kv = mla_w_kv_up[j].reshape(kv_lora, heads, MLA_NOPE + MLA_V)
            w_knt = jnp.transpose(w_kv[..., :MLA_NOPE], (1, 2, 0)).reshape(heads * MLA_NOPE, kv_lora)
            w_v = w_kv[..., MLA_NOPE:].reshape(kv_lora, heads * MLA_V)
            q_gain = _pad_qk_channels(mla_q_norm[j].reshape(1, MLA_QK), 1).astype(F32) * MLA_Q_SCALE
            k_gain = _pad_qk_channels(mla_k_norm[j].reshape(1, MLA_QK), 1).astype(F32)
            k_gain_t = jnp.broadcast_to(k_gain.reshape(MLA_QK_PAD, 1), (MLA_QK_PAD, mla_tm))
            bound = 1.02 * MLA_QK * MLA_Q_SCALE * jnp.max(jnp.abs(mla_q_norm[j])) * jnp.max(jnp.abs(mla_k_norm[j]))
            qh, kth, vh = _mla_in(
                x, cos_t, sin_t, cos_tt, sin_tt, gain, w_in.astype(BF16),
                mla_q_a_norm[j].reshape(1, q_lora).astype(F32),
                _pad_qk_channels(mla_w_q_up[j], heads).astype(BF16),
                mla_kv_a_norm[j].reshape(1, kv_lora).astype(F32),
                w_knt.astype(BF16), w_v.astype(BF16),
                q_gain, k_gain_t[:LANES], k_gain_t[LANES:],
                heads=heads, tm=mla_tm)
            o = lax.cond(
                bound <= MLA_SCORE_MAX,
                functools.partial(_attention, tq=512, hg=2, bounded=True),
                functools.partial(_attention, tq=512, hg=2, bounded=False),
                qh, kth, vh)
            w_out = mla_w_out[j]
        x = _ffn(x, o, w_out.astype(BF16), norm_ffn[layer].reshape(1, d).astype(F32),
                 ffn_w_up[layer].astype(BF16), ffn_conv_w[layer].astype(F32),
                 ffn_conv_b[layer].reshape(1, -1).astype(F32), ffn_w_down[layer].astype(BF16),
                 tm=512, fc=256)
    return x
```
